```python
import math
import jax, jax.numpy as jnp
from jax import lax
import numpy as np

D_MODEL = 1024
BATCH = 8
SEQ = 2048
DEPTH = 2

CHUNK = 64
EPS = 1e-6
A_HEADS = 8
A_DK = 128
A_DV = 128
A_QK = A_HEADS * A_DK
A_VW = A_HEADS * A_DV
CONV_K = 4
A_CONV_CH = 2 * A_QK + A_VW
B_HEADS = 16
B_DH = 64
B_W = B_HEADS * B_DH
BAND_CHUNKS = 8
BAND_LEN = (BAND_CHUNKS + 1) * CHUNK
MAX_REL = 256
N_REL = (CHUNK - 1) + MAX_REL + 1
MEM_LEN = 256
M_HEADS = 4
M_DH = D_MODEL // M_HEADS
D_FF = ((8 * D_MODEL + 3 * 256 - 1) // (3 * 256)) * 256
IN_SPLITS = (A_QK, A_QK, A_VW, A_VW, A_HEADS, A_HEADS, B_W, B_W, B_W, D_MODEL, D_MODEL)
N_IN = sum(IN_SPLITS)

kernel_name = "hybrid_stream_delta_band_block"


def rmsnorm(x, g):
    xf = x.astype(jnp.float32)
    y = xf * lax.rsqrt(jnp.mean(xf * xf, axis=-1, keepdims=True) + EPS)
    return (y * g.astype(jnp.float32)).astype(x.dtype)


def _l2norm(x):
    return x * lax.rsqrt(jnp.sum(x * x, axis=-1, keepdims=True) + EPS)


def _split_cols(t, sizes):
    outs, start = [], 0
    for n in sizes:
        outs.append(t[..., start:start + n])
        start += n
    return outs


def _to_chunks(t, h, d):
    b, s, _ = t.shape
    return t.reshape(b, s // CHUNK, CHUNK, h, d).transpose(0, 3, 1, 2, 4)


def _heads_to_chunks(t):
    b, s, h = t.shape
    return t.reshape(b, s // CHUNK, CHUNK, h).transpose(0, 3, 1, 2)


def _causal_dwconv(t, w):
    return lax.conv_general_dilated(
        t, w[:, None, :], window_strides=(1,), padding=[(CONV_K - 1, 0)],
        dimension_numbers=("NWC", "WIO", "NWC"), feature_group_count=t.shape[-1])


def _gated_deltanet(q, k, v, alpha, beta, a_log, dt_bias):
    dtype = v.dtype
    b, s, _ = v.shape
    f32 = jnp.float32
    q = _l2norm(_to_chunks(q.astype(f32), A_HEADS, A_DK)) * (A_DK ** -0.5)
    k = _l2norm(_to_chunks(k.astype(f32), A_HEADS, A_DK))
    v = _to_chunks(v.astype(f32), A_HEADS, A_DV)
    beta = _heads_to_chunks(jax.nn.sigmoid(beta.astype(f32)))
    g = -jnp.exp(a_log.astype(f32)) * jax.nn.softplus(alpha.astype(f32) + dt_bias.astype(f32))
    G = jnp.cumsum(_heads_to_chunks(g), axis=-1)
    incl = jnp.tril(jnp.ones((CHUNK, CHUNK), dtype=bool))
    strict = jnp.tril(jnp.ones((CHUNK, CHUNK), dtype=bool), -1)
    diff = G[..., :, None] - G[..., None, :]
    decay = jnp.where(incl, jnp.exp(jnp.where(incl, diff, 0.0)), 0.0)
    kk = jnp.einsum("bhncd,bhnjd->bhncj", k, k)
    L = jnp.where(strict, beta[..., :, None] * kk * decay, 0.0)
    rhs = jnp.concatenate([beta[..., None] * v, (beta * jnp.exp(G))[..., None] * k], axis=-1)
    sol = lax.linalg.triangular_solve(jnp.eye(CHUNK, dtype=f32) + L, rhs, left_side=True,
                                      lower=True, unit_diagonal=True)
    u_t, w = sol[..., :A_DV], sol[..., A_DV:]
    p_intra = jnp.einsum("bhncd,bhnjd->bhncj", q, k) * decay
    q_dec = q * jnp.exp(G)[..., None]
    g_last = G[..., -1]
    k_dec = k * jnp.exp(g_last[..., None] - G)[..., None]
    xs = tuple(jnp.moveaxis(t, 2, 0) for t in (u_t, w, p_intra, q_dec, k_dec, jnp.exp(g_last)))

    def step(S, inp):
        ut, wt, pt, qt, kt, et = inp
        U = ut - jnp.einsum("bhck,bhkv->bhcv", wt, S)
        o = jnp.einsum("bhck,bhkv->bhcv", qt, S) + jnp.einsum("bhcj,bhjv->bhcv", pt, U)
        S = S * et[..., None, None] + jnp.einsum("bhck,bhcv->bhkv", kt, U)
        return S, o

    S0 = jnp.zeros((b, A_HEADS, A_DK, A_DV), f32)
    _, o = lax.scan(step, S0, xs)
    return o.transpose(1, 0, 3, 2, 4).reshape(b, s, A_HEADS, A_DV).astype(dtype)


def _band_bias(rel_bias):
    i = jnp.arange(CHUNK)[:, None]
    kpos = jnp.arange(BAND_LEN)[None, :]
    r = (BAND_CHUNKS - kpos // CHUNK) * CHUNK + i - kpos % CHUNK
    idx = jnp.clip(r, -(CHUNK - 1), MAX_REL) + (CHUNK - 1)
    return rel_bias[:, idx].astype(jnp.float32)


def _chunk_band_attention(q, k, v, bias):
    b, s, _ = q.shape
    q = _to_chunks(q, B_HEADS, B_DH)
    pad = ((0, 0), (0, 0), (BAND_CHUNKS, 0), (0, 0), (0, 0))
    kp = jnp.pad(_to_chunks(k, B_HEADS, B_DH), pad)
    vp = jnp.pad(_to_chunks(v, B_HEADS, B_DH), pad)
    key_chunk_off = jnp.repeat(jnp.arange(BAND_CHUNKS + 1), CHUNK) - BAND_CHUNKS
    scale = B_DH ** -0.5

    def one_chunk(n):
        qn = lax.dynamic_index_in_dim(q, n, axis=2, keepdims=False)
        kb = lax.dynamic_slice_in_dim(kp, n, BAND_CHUNKS + 1, axis=2).reshape(b, B_HEADS, BAND_LEN, B_DH)
        vb = lax.dynamic_slice_in_dim(vp, n, BAND_CHUNKS + 1, axis=2).reshape(b, B_HEADS, BAND_LEN, B_DH)
        sc = jnp.einsum("bhqd,bhkd->bhqk", qn, kb).astype(jnp.float32) * scale + bias
        sc = jnp.where((n + key_chunk_off) >= 0, sc, -1e30)
        p = jax.nn.softmax(sc, axis=-1).astype(vb.dtype)
        return jnp.einsum("bhqk,bhkd->bhqd", p, vb)

    o = lax.map(one_chunk, jnp.arange(s // CHUNK))
    return o.transpose(1, 0, 3, 2, 4).reshape(b, s, B_W)


def _memory_xattn(h, memn, w_q, w_kv, w_out):
    b, s, _ = h.shape
    m = memn.shape[1]
    q = (h @ w_q).reshape(b, s, M_HEADS, M_DH)
    kk, vv = _split_cols(memn @ w_kv, (M_HEADS * M_DH, M_HEADS * M_DH))
    kk = kk.reshape(b, m, M_HEADS, M_DH)
    vv = vv.reshape(b, m, M_HEADS, M_DH)
    sc = jnp.einsum("bqhd,bkhd->bhqk", q, kk).astype(jnp.float32) * (M_DH ** -0.5)
    p = jax.nn.softmax(sc, axis=-1).astype(vv.dtype)
    o = jnp.einsum("bhqk,bkhd->bqhd", p, vv).reshape(b, s, M_HEADS * M_DH)
    return o @ w_out


def setup_inputs(seed: int = 0) -> dict:
    key = jax.random.key(seed)
    ks = jax.random.split(key, 24)
    f32 = jnp.float32

    def nrm(k, shape, scale):
        return jax.random.normal(k, shape, f32) * scale

    def gain(k, shape):
        return 1.0 + 0.02 * jax.random.normal(k, shape, f32)

    dt = jnp.exp(jax.random.uniform(ks[6], (DEPTH, A_HEADS), f32, math.log(1e-3), math.log(1e-1)))
    return {
        "x": nrm(ks[0], (BATCH, SEQ, D_MODEL), 1.0),
        "mem": nrm(ks[1], (BATCH, MEM_LEN, D_MODEL), 1.0),
        "norm_mix": gain(ks[2], (DEPTH, D_MODEL)),
        "w_in": nrm(ks[3], (DEPTH, D_MODEL, N_IN), D_MODEL ** -0.5),
        "conv_w": nrm(ks[4], (DEPTH, CONV_K, A_CONV_CH), CONV_K ** -0.5),
        "a_log": jnp.log(jax.random.uniform(ks[5], (DEPTH, A_HEADS), f32, 1.0, 16.0)),
        "dt_bias": dt + jnp.log(-jnp.expm1(-dt)),
        "head_norm": gain(ks[7], (DEPTH, A_DV)),
        "w_a_out": nrm(ks[8], (DEPTH, A_VW, D_MODEL), A_VW ** -0.5),
        "w_b_out": nrm(ks[9], (DEPTH, B_W, D_MODEL), B_W ** -0.5),
        "rel_bias": nrm(ks[10], (B_HEADS, N_REL), 0.5),
        "w_o": nrm(ks[11], (DEPTH, D_MODEL, D_MODEL), D_MODEL ** -0.5),
        "norm_xattn": gain(ks[12], (DEPTH, D_MODEL)),
        "norm_mem": gain(ks[13], (DEPTH, D_MODEL)),
        "w_mq": nrm(ks[14], (DEPTH, D_MODEL, M_HEADS * M_DH), D_MODEL ** -0.5),
        "w_mkv": nrm(ks[15], (DEPTH, D_MODEL, 2 * M_HEADS * M_DH), D_MODEL ** -0.5),
        "w_mo": nrm(ks[16], (DEPTH, M_HEADS * M_DH, D_MODEL), (M_HEADS * M_DH) ** -0.5),
        "norm_ffn": gain(ks[17], (DEPTH, D_MODEL)),
        "w_gate_up": nrm(ks[18], (DEPTH, D_MODEL, 2 * D_FF), D_MODEL ** -0.5),
        "w_down": nrm(ks[19], (DEPTH, D_FF, D_MODEL), D_FF ** -0.5),
        "norm_final": gain(ks[20], (D_MODEL,)),
    }


def reference(x, mem, norm_mix, w_in, conv_w, a_log, dt_bias, head_norm, w_a_out, w_b_out,
              rel_bias, w_o, norm_xattn, norm_mem, w_mq, w_mkv, w_mo, norm_ffn, w_gate_up,
              w_down, norm_final):
    b, s, _ = x.shape
    band_bias = _band_bias(rel_bias)
    for l in range(DEPTH):
        h = rmsnorm(x, norm_mix[l])
        qa, ka, va, za, alpha, beta, qb, kb, vb, ga, gb = _split_cols(h @ w_in[l], IN_SPLITS)
        qkv = jax.nn.silu(_causal_dwconv(jnp.concatenate([qa, ka, va], axis=-1), conv_w[l]))
        qa, ka, va = _split_cols(qkv, (A_QK, A_QK, A_VW))
        oa = _gated_deltanet(qa, ka, va, alpha, beta, a_log[l], dt_bias[l])
        oa = rmsnorm(oa, head_norm[l]) * jax.nn.silu(za.reshape(b, s, A_HEADS, A_DV))
        oa = oa.reshape(b, s, A_VW)
        ob = _chunk_band_attention(qb, kb, vb, band_bias)
        y = jax.nn.sigmoid(ga) * (oa @ w_a_out[l]) + jax.nn.sigmoid(gb) * (ob @ w_b_out[l])
        x = x + y @ w_o[l]
        x = x + _memory_xattn(rmsnorm(x, norm_xattn[l]), rmsnorm(mem, norm_mem[l]),
                              w_mq[l], w_mkv[l], w_mo[l])
        gate, up = _split_cols(rmsnorm(x, norm_ffn[l]) @ w_gate_up[l], (D_FF, D_FF))
        x = x + (jax.nn.silu(gate) * up) @ w_down[l]
    return rmsnorm(x, norm_final)
```

```python
import functools

import jax
import jax.numpy as jnp
from jax import lax
from jax.experimental import pallas as pl
from jax.experimental.pallas import tpu as pltpu

F32 = jnp.float32
BF16 = jnp.bfloat16
HI = lax.Precision.HIGHEST

D_MODEL = 1024
CHUNK = 64
EPS = 1e-6
A_HEADS = 8
A_DK = 128
A_W = A_HEADS * A_DK
CONV_K = 4
B_HEADS = 16
B_DH = 64
BAND_CHUNKS = 8
BAND_LEN = (BAND_CHUNKS + 1) * CHUNK
MAX_REL = 256
N_REL = (CHUNK - 1) + MAX_REL + 1
N_REL_PAD = 384
M_HEADS = 4
M_DH = D_MODEL // M_HEADS
D_FF = 2816
AB_PAD = 128
QBLK = BAND_CHUNKS * CHUNK

COL_QKV_A, COL_ZA, COL_QB, COL_KB, COL_VB, COL_GA, COL_GB = 0, 3, 4, 5, 6, 7, 8
N_MAIN = 9 * D_MODEL

VMEM_LIMIT = 56 * 1024 * 1024


def _params(sem):
    return pltpu.CompilerParams(dimension_semantics=sem, vmem_limit_bytes=VMEM_LIMIT)


def _const_spec(shape):
    nd = len(shape)
    return pl.BlockSpec(shape, lambda *_: (0,) * nd, pipeline_mode=pl.Buffered(1))


def _dot(a, b):
    return jnp.dot(a.astype(BF16), b.astype(BF16), preferred_element_type=F32)


def _dot_nt(a, b):
    return lax.dot_general(a.astype(BF16), b.astype(BF16), (((1,), (1,)), ((), ())),
                           preferred_element_type=F32)


def _dot_tn(a, b):
    return lax.dot_general(a.astype(BF16), b.astype(BF16), (((0,), (0,)), ((), ())),
                           preferred_element_type=F32)


def _dot_hi(a, b):
    return jnp.dot(a, b, precision=HI, preferred_element_type=F32)


def _sigmoid(x):
    return 1.0 / (1.0 + jnp.exp(-x))


def _softplus(x):
    return jnp.maximum(x, 0.0) + jnp.log1p(jnp.exp(-jnp.abs(x)))


def _rmsnorm(x, g):
    return x * lax.rsqrt(jnp.mean(x * x, axis=-1, keepdims=True) + EPS) * g


def _inproj_kernel(x_ref, g_ref, w_ref, wab_ref, o_ref, ab_ref, h_ref):
    @pl.when(pl.program_id(1) == 0)
    def _():
        h = _rmsnorm(x_ref[...], g_ref[...]).astype(BF16)
        h_ref[...] = h
        ab_ref[...] = jnp.dot(h, wab_ref[...], preferred_element_type=F32)

    o_ref[...] = jnp.dot(h_ref[...], w_ref[...], preferred_element_type=F32).astype(o_ref.dtype)


def _inproj(x, g, w_main, w_ab, tm, tn):
    t = x.shape[0]
    n = w_main.shape[1]
    return pl.pallas_call(
        _inproj_kernel,
        grid=(t // tm, n // tn),
        in_specs=[
            pl.BlockSpec((tm, D_MODEL), lambda i, j: (i, 0)),
            pl.BlockSpec((1, D_MODEL), lambda i, j: (0, 0)),
            pl.BlockSpec((D_MODEL, tn), lambda i, j: (0, j)),
            pl.BlockSpec((D_MODEL, AB_PAD), lambda i, j: (0, 0)),
        ],
        out_specs=[
            pl.BlockSpec((tm, tn), lambda i, j: (i, j)),
            pl.BlockSpec((tm, AB_PAD), lambda i, j: (i, 0)),
        ],
        out_shape=[
            jax.ShapeDtypeStruct((t, n), BF16),
            jax.ShapeDtypeStruct((t, AB_PAD), F32),
        ],
        scratch_shapes=[pltpu.VMEM((tm, D_MODEL), BF16)],
        compiler_params=_params(("parallel", "arbitrary")),
        name="inproj",
    )(x, g, w_main, w_ab)


def _norm_matmul_kernel(x_ref, g_ref, w_ref, o_ref):
    h = _rmsnorm(x_ref[...], g_ref[...]).astype(BF16)
    o_ref[...] = jnp.dot(h, w_ref[...], preferred_element_type=F32).astype(o_ref.dtype)


def _norm_matmul(x, g, w, tm):
    t = x.shape[0]
    n = w.shape[1]
    return pl.pallas_call(
        _norm_matmul_kernel,
        grid=(t // tm,),
        in_specs=[
            pl.BlockSpec((tm, D_MODEL), lambda i: (i, 0)),
            _const_spec((1, D_MODEL)),
            _const_spec((D_MODEL, n)),
        ],
        out_specs=pl.BlockSpec((tm, n), lambda i: (i, 0)),
        out_shape=jax.ShapeDtypeStruct((t, n), BF16),
        compiler_params=_params(("parallel",)),
        name="mem_kv",
    )(x, g, w)


def _unit_lower_inverse(lmat, row, col):
    eye = (row == col).astype(F32)
    same16 = (row >> 4) == (col >> 4)
    same32 = (row >> 5) == (col >> 5)
    d1 = jnp.where(same16, lmat, 0.0)
    e16 = jnp.where(jnp.logical_and(same32, jnp.logical_not(same16)), lmat, 0.0)
    e32 = jnp.where(same32, 0.0, lmat)
    d2 = _dot_hi(d1, d1)
    d4 = _dot_hi(d2, d2)
    d8 = _dot_hi(d4, d4)
    r = eye - d1
    r = r + _dot_hi(r, d2)
    r = r + _dot_hi(r, d4)
    r = r + _dot_hi(r, d8)
    r = r - _dot_hi(r, _dot_hi(e16, r))
    r = r - _dot_hi(r, _dot_hi(e32, r))
    return r


def _delta_kernel(qkv_ref, za_ref, ab_ref, alt_ref, cw_ref, alog_r_ref, dt_r_ref,
                  alog_c_ref, dt_c_ref, hn_ref, o_ref, s_ref, tail_ref):
    @pl.when(pl.program_id(1) == 0)
    def _():
        s_ref[...] = jnp.zeros_like(s_ref)
        tail_ref[...] = jnp.zeros_like(tail_ref)

    xin = qkv_ref[...].astype(F32)
    xx = jnp.concatenate([tail_ref[...], xin], axis=0)
    tail_ref[...] = xin[CHUNK - 8:, :]
    cw = cw_ref[...]
    acc = xin * cw[CONV_K - 1:CONV_K, :]
    for s in range(1, CONV_K):
        shifted = pltpu.roll(xx, s, axis=0)[8:, :]
        acc = acc + shifted * cw[CONV_K - 1 - s:CONV_K - s, :]
    qkv = acc * _sigmoid(acc)

    row = lax.broadcasted_iota(jnp.int32, (CHUNK, CHUNK), 0)
    col = lax.broadcasted_iota(jnp.int32, (CHUNK, CHUNK), 1)
    incl = row >= col
    strict = row > col

    ab = ab_ref[...]
    g_c = -jnp.exp(alog_r_ref[...]) * _softplus(ab + dt_r_ref[...])
    cum_c = _dot_hi(incl.astype(F32), g_c)
    g_r = -jnp.exp(alog_c_ref[...]) * _softplus(alt_ref[0] + dt_c_ref[...])
    cum_r = _dot_hi(g_r, (row <= col).astype(F32))
    beta_c = _sigmoid(ab)
    exp_cum = jnp.exp(cum_c)
    g_last = cum_c[CHUNK - 1:CHUNK, :]
    kdec_scale = jnp.exp(g_last - cum_c)
    e_last = jnp.exp(g_last)

    for h in range(A_HEADS):
        lanes = slice(h * A_DK, (h + 1) * A_DK)
        q = qkv[:, h * A_DK:(h + 1) * A_DK]
        k = qkv[:, A_W + h * A_DK:A_W + (h + 1) * A_DK]
        v = qkv[:, 2 * A_W + h * A_DK:2 * A_W + (h + 1) * A_DK]
        qn = q * lax.rsqrt(jnp.sum(q * q, axis=-1, keepdims=True) + EPS) * (A_DK ** -0.5)
        kn = k * lax.rsqrt(jnp.sum(k * k, axis=-1, keepdims=True) + EPS)

        diff = cum_c[:, h:h + 1] - cum_r[h:h + 1, :]
        decay = jnp.where(incl, jnp.exp(jnp.where(incl, diff, 0.0)), 0.0)
        beta = beta_c[:, A_HEADS + h:A_HEADS + h + 1]
        eg = exp_cum[:, h:h + 1]

        kk = _dot_nt(kn, kn)
        lmat = jnp.where(strict, beta * kk * decay, 0.0)
        tinv = _unit_lower_inverse(lmat, row, col)
        rhs = jnp.concatenate([beta * v, (beta * eg) * kn], axis=1)
        sol = _dot_hi(tinv, rhs)
        u_t = sol[:, :A_DK]
        w = sol[:, A_DK:]

        p = _dot_nt(qn, kn) * decay
        q_dec = qn * eg
        k_dec = kn * kdec_scale[:, h:h + 1]

        s_bf = s_ref[h].astype(BF16)
        wq_s = _dot(jnp.concatenate([w, q_dec], axis=0), s_bf)
        u = u_t - wq_s[:CHUNK]
        o = wq_s[CHUNK:] + _dot(p, u)
        s_ref[h] = s_ref[h] * e_last[:, h:h + 1] + _dot_tn(k_dec, u)

        on = _rmsnorm(o, hn_ref[...])
        z = za_ref[:, lanes].astype(F32)
        o_ref[:, lanes] = (on * (z * _sigmoid(z))).astype(o_ref.dtype)


def _delta(p_main, ab, alt, conv_w, alog_r, dt_r, alog_c, dt_c, head_norm, batch, nc):
    t = p_main.shape[0]
    return pl.pallas_call(
        _delta_kernel,
        grid=(batch, nc),
        in_specs=[
            pl.BlockSpec((CHUNK, 3 * A_W), lambda b, n: (b * nc + n, COL_QKV_A)),
            pl.BlockSpec((CHUNK, A_W), lambda b, n: (b * nc + n, COL_ZA)),
            pl.BlockSpec((CHUNK, AB_PAD), lambda b, n: (b * nc + n, 0)),
            pl.BlockSpec((1, A_HEADS, CHUNK), lambda b, n: (b * nc + n, 0, 0)),
            _const_spec((CONV_K, 3 * A_W)),
            _const_spec((1, AB_PAD)),
            _const_spec((1, AB_PAD)),
            _const_spec((A_HEADS, 1)),
            _const_spec((A_HEADS, 1)),
            _const_spec((1, A_DK)),
        ],
        out_specs=pl.BlockSpec((CHUNK, A_W), lambda b, n: (b * nc + n, 0)),
        out_shape=jax.ShapeDtypeStruct((t, A_W), BF16),
        scratch_shapes=[
            pltpu.VMEM((A_HEADS, A_DK, A_DK), F32),
            pltpu.VMEM((8, 3 * A_W), F32),
        ],
        compiler_params=_params(("parallel", "arbitrary")),
        name="delta",
    )(p_main, p_main, ab, alt, conv_w, alog_r, dt_r, alog_c, dt_c, head_norm)


def _band_bias_kernel(relt_ref, o_ref):
    rows = o_ref.shape[0] // BAND_LEN
    shape = (rows, BAND_LEN, N_REL_PAD)
    i = lax.broadcasted_iota(jnp.int32, shape, 0) + pl.program_id(0) * rows
    kpos = lax.broadcasted_iota(jnp.int32, shape, 1)
    cand = lax.broadcasted_iota(jnp.int32, shape, 2)
    r = (BAND_CHUNKS - (kpos >> 6)) * CHUNK + i - (kpos & (CHUNK - 1))
    idx = jnp.clip(r, -(CHUNK - 1), MAX_REL) + (CHUNK - 1)
    onehot = (idx == cand).astype(BF16).reshape(rows * BAND_LEN, N_REL_PAD)
    rel = relt_ref[...]
    r1 = rel.astype(BF16)
    rem = rel - r1.astype(F32)
    r2 = rem.astype(BF16)
    r3 = (rem - r2.astype(F32)).astype(BF16)
    acc = jnp.dot(onehot, r1, preferred_element_type=F32)
    acc = acc + jnp.dot(onehot, r2, preferred_element_type=F32)
    acc = acc + jnp.dot(onehot, r3, preferred_element_type=F32)
    o_ref[...] = acc


def _band_bias(rel_bias):
    relt = jnp.pad(rel_bias.T, ((0, N_REL_PAD - N_REL), (0, 0)))
    rows = 8
    out = pl.pallas_call(
        _band_bias_kernel,
        grid=(CHUNK // rows,),
        in_specs=[_const_spec((N_REL_PAD, B_HEADS))],
        out_specs=pl.BlockSpec((rows * BAND_LEN, B_HEADS), lambda g: (g, 0)),
        out_shape=jax.ShapeDtypeStruct((CHUNK * BAND_LEN, B_HEADS), F32),
        compiler_params=_params(("parallel",)),
        name="band_bias",
    )(relt)
    return out.reshape(CHUNK, BAND_LEN, B_HEADS).transpose(2, 0, 1)


def _band_kernel(q_ref, kp_ref, kc_ref, vp_ref, vc_ref, bias_ref, o_ref, kwin, vwin):
    blk = pl.program_id(1)
    kwin[:QBLK, :] = kp_ref[...]
    kwin[QBLK:, :] = kc_ref[...]
    vwin[:QBLK, :] = vp_ref[...]
    vwin[QBLK:, :] = vc_ref[...]
    col_chunk = lax.broadcasted_iota(jnp.int32, (CHUNK, BAND_LEN), 1) >> 6
    scale = B_DH ** -0.5

    def body(a, carry):
        r0 = pl.multiple_of(a * CHUNK, CHUNK)
        valid = (col_chunk + a + BAND_CHUNKS * blk) >= BAND_CHUNKS
        outs = []
        for h in range(B_HEADS):
            lanes = slice(h * B_DH, (h + 1) * B_DH)
            qh = q_ref[pl.ds(r0, CHUNK), lanes]
            kh = kwin[pl.ds(r0, BAND_LEN), lanes]
            vh = vwin[pl.ds(r0, BAND_LEN), lanes]
            sc = _dot_nt(qh, kh) * scale + bias_ref[h]
            sc = jnp.where(valid, sc, -1e30)
            m = jnp.max(sc, axis=-1, keepdims=True)
            p = jnp.exp(sc - m)
            denom = jnp.sum(p, axis=-1, keepdims=True)
            outs.append(_dot(p, vh) / denom)
        o_ref[pl.ds(r0, CHUNK), :] = jnp.concatenate(outs, axis=1).astype(o_ref.dtype)
        return carry

    lax.fori_loop(0, QBLK // CHUNK, body, 0)


def _band(p_main, bias, batch, nblk):
    t = p_main.shape[0]

    def prev(b, i):
        return b * nblk + jnp.maximum(i - 1, 0)

    return pl.pallas_call(
        _band_kernel,
        grid=(batch, nblk),
        in_specs=[
            pl.BlockSpec((QBLK, D_MODEL), lambda b, i: (b * nblk + i, COL_QB)),
            pl.BlockSpec((QBLK, D_MODEL), lambda b, i: (prev(b, i), COL_KB)),
            pl.BlockSpec((QBLK, D_MODEL), lambda b, i: (b * nblk + i, COL_KB)),
            pl.BlockSpec((QBLK, D_MODEL), lambda b, i: (prev(b, i), COL_VB)),
            pl.BlockSpec((QBLK, D_MODEL), lambda b, i: (b * nblk + i, COL_VB)),
            _const_spec((B_HEADS, CHUNK, BAND_LEN)),
        ],
        out_specs=pl.BlockSpec((QBLK, D_MODEL), lambda b, i: (b * nblk + i, 0)),
        out_shape=jax.ShapeDtypeStruct((t, D_MODEL), BF16),
        scratch_shapes=[
            pltpu.VMEM((2 * QBLK, D_MODEL), BF16),
            pltpu.VMEM((2 * QBLK, D_MODEL), BF16),
        ],
        compiler_params=_params(("parallel", "parallel")),
        name="band",
    )(p_main, p_main, p_main, p_main, p_main, bias)


def _merge_kernel(oa_ref, ob_ref, ga_ref, gb_ref, x_ref, wa_ref, wb_ref, wo_ref, o_ref):
    ya = jnp.dot(oa_ref[...], wa_ref[...], preferred_element_type=F32)
    yb = jnp.dot(ob_ref[...], wb_ref[...], preferred_element_type=F32)
    y = _sigmoid(ga_ref[...].astype(F32)) * ya + _sigmoid(gb_ref[...].astype(F32)) * yb
    o_ref[...] = x_ref[...] + jnp.dot(y.astype(BF16), wo_ref[...], preferred_element_type=F32)


def _merge(oa, ob, p_main, x, wa, wb, wo, tm):
    t = x.shape[0]
    row = lambda i: (i, 0)
    return pl.pallas_call(
        _merge_kernel,
        grid=(t // tm,),
        in_specs=[
            pl.BlockSpec((tm, D_MODEL), row),
            pl.BlockSpec((tm, D_MODEL), row),
            pl.BlockSpec((tm, D_MODEL), lambda i: (i, COL_GA)),
            pl.BlockSpec((tm, D_MODEL), lambda i: (i, COL_GB)),
            pl.BlockSpec((tm, D_MODEL), row),
            _const_spec((D_MODEL, D_MODEL)),
            _const_spec((D_MODEL, D_MODEL)),
            _const_spec((D_MODEL, D_MODEL)),
        ],
        out_specs=pl.BlockSpec((tm, D_MODEL), row),
        out_shape=jax.ShapeDtypeStruct((t, D_MODEL), F32),
        compiler_params=_params(("parallel",)),
        name="merge",
    )(oa, ob, p_main, p_main, x, wa, wb, wo)


def _xattn_kernel(x_ref, g_ref, wq_ref, kv_ref, wo_ref, o_ref):
    x = x_ref[...]
    h = _rmsnorm(x, g_ref[...]).astype(BF16)
    q = jnp.dot(h, wq_ref[...], preferred_element_type=F32).astype(BF16)
    scale = M_DH ** -0.5
    outs = []
    for hh in range(M_HEADS):
        kh = kv_ref[:, hh * M_DH:(hh + 1) * M_DH]
        vh = kv_ref[:, D_MODEL + hh * M_DH:D_MODEL + (hh + 1) * M_DH]
        sc = _dot_nt(q[:, hh * M_DH:(hh + 1) * M_DH], kh) * scale
        m = jnp.max(sc, axis=-1, keepdims=True)
        p = jnp.exp(sc - m)
        denom = jnp.sum(p, axis=-1, keepdims=True)
        outs.append((_dot(p, vh) / denom).astype(BF16))
    o = jnp.concatenate(outs, axis=1)
    o_ref[...] = x + jnp.dot(o, wo_ref[...], preferred_element_type=F32)


def _xattn(x, g, wq, kv, wo, tm, seq, mem_len):
    t = x.shape[0]
    per_batch = seq // tm
    return pl.pallas_call(
        _xattn_kernel,
        grid=(t // tm,),
        in_specs=[
            pl.BlockSpec((tm, D_MODEL), lambda i: (i, 0)),
            _const_spec((1, D_MODEL)),
            _const_spec((D_MODEL, D_MODEL)),
            pl.BlockSpec((mem_len, 2 * D_MODEL), lambda i: (i // per_batch, 0)),
            _const_spec((D_MODEL, D_MODEL)),
        ],
        out_specs=pl.BlockSpec((tm, D_MODEL), lambda i: (i, 0)),
        out_shape=jax.ShapeDtypeStruct((t, D_MODEL), F32),
        compiler_params=_params(("parallel",)),
        name="xattn",
    )(x, g, wq, kv, wo)


FF_CHUNK = 256


def _ffn_kernel(x_ref, g_ref, wgu_ref, wd_ref, gf_ref, o_ref, act_ref, *, final_norm):
    x = x_ref[...]
    h = _rmsnorm(x, g_ref[...]).astype(BF16)
    for c in range(D_FF // FF_CHUNK):
        lo = c * FF_CHUNK
        gate = jnp.dot(h, wgu_ref[:, lo:lo + FF_CHUNK], preferred_element_type=F32)
        up = jnp.dot(h, wgu_ref[:, D_FF + lo:D_FF + lo + FF_CHUNK], preferred_element_type=F32)
        act_ref[:, lo:lo + FF_CHUNK] = (gate * _sigmoid(gate) * up).astype(BF16)
    y = x + jnp.dot(act_ref[...], wd_ref[...], preferred_element_type=F32)
    if final_norm:
        y = _rmsnorm(y, gf_ref[...])
    o_ref[...] = y


def _ffn(x, g, wgu, wd, gf, tm, final_norm):
    t = x.shape[0]
    return pl.pallas_call(
        functools.partial(_ffn_kernel, final_norm=final_norm),
        grid=(t // tm,),
        in_specs=[
            pl.BlockSpec((tm, D_MODEL), lambda i: (i, 0)),
            _const_spec((1, D_MODEL)),
            _const_spec((D_MODEL, 2 * D_FF)),
            _const_spec((D_FF, D_MODEL)),
            _const_spec((1, D_MODEL)),
        ],
        out_specs=pl.BlockSpec((tm, D_MODEL), lambda i: (i, 0)),
        out_shape=jax.ShapeDtypeStruct((t, D_MODEL), F32),
        scratch_shapes=[pltpu.VMEM((tm, D_FF), BF16)],
        compiler_params=_params(("parallel",)),
        name="ffn",
    )(x, g, wgu, wd, gf)


def kernel(x, mem, norm_mix, w_in, conv_w, a_log, dt_bias, head_norm, w_a_out, w_b_out,
           rel_bias, w_o, norm_xattn, norm_mem, w_mq, w_mkv, w_mo, norm_ffn, w_gate_up,
           w_down, norm_final):
    batch, seq, d = x.shape
    mem_len = mem.shape[1]
    depth = w_in.shape[0]
    t = batch * seq
    nc = seq // CHUNK
    nblk = seq // QBLK
    tm_in = min(1024, t)
    tm = min(512, seq)

    xf = x.reshape(t, d)
    memf = mem.reshape(batch * mem_len, d)
    bias = _band_bias(rel_bias)
    ab_lo = 4 * A_W
    ab_hi = ab_lo + 2 * A_HEADS

    def pad_lanes(v):
        return jnp.pad(v.reshape(1, -1), ((0, 0), (0, AB_PAD - v.shape[-1])))

    for l in range(depth):
        w_l = w_in[l]
        w_main = jnp.concatenate([w_l[:, :ab_lo], w_l[:, ab_hi:]], axis=1).astype(BF16)
        w_ab = jnp.pad(w_l[:, ab_lo:ab_hi], ((0, 0), (0, AB_PAD - 2 * A_HEADS))).astype(BF16)
        p_main, ab = _inproj(xf, norm_mix[l].reshape(1, d), w_main, w_ab, tm_in, D_MODEL)

        alt = ab[:, :A_HEADS].reshape(batch * nc, CHUNK, A_HEADS).transpose(0, 2, 1)
        oa = _delta(p_main, ab, alt, conv_w[l], pad_lanes(a_log[l]), pad_lanes(dt_bias[l]),
                    a_log[l].reshape(A_HEADS, 1), dt_bias[l].reshape(A_HEADS, 1),
                    head_norm[l].reshape(1, A_DK), batch, nc)
        ob = _band(p_main, bias, batch, nblk)
        x1 = _merge(oa, ob, p_main, xf, w_a_out[l].astype(BF16), w_b_out[l].astype(BF16),
                    w_o[l].astype(BF16), tm)

        kv = _norm_matmul(memf, norm_mem[l].reshape(1, d), w_mkv[l].astype(BF16),
                          min(512, batch * mem_len))
        x2 = _xattn(x1, norm_xattn[l].reshape(1, d), w_mq[l].astype(BF16), kv,
                    w_mo[l].astype(BF16), tm, seq, mem_len)

        xf = _ffn(x2, norm_ffn[l].reshape(1, d), w_gate_up[l].astype(BF16),
                  w_down[l].astype(BF16), norm_final.reshape(1, d), tm,
                  final_norm=(l == depth - 1))
    return xf.reshape(batch, seq, d)
```

```python
import functools

import jax
import jax.numpy as jnp
from jax import lax
from jax.experimental import pallas as pl
from jax.experimental.pallas import tpu as pltpu

F32 = jnp.float32
BF16 = jnp.bfloat16
HI = lax.Precision.HIGHEST

D_MODEL = 1024
CHUNK = 64
EPS = 1e-6
A_HEADS = 8
A_DK = 128
A_W = A_HEADS * A_DK
CONV_K = 4
B_HEADS = 16
B_DH = 64
BAND_CHUNKS = 8
BAND_LEN = (BAND_CHUNKS + 1) * CHUNK
MAX_REL = 256
N_REL = (CHUNK - 1) + MAX_REL + 1
N_REL_PAD = 384
M_HEADS = 4
M_DH = D_MODEL // M_HEADS
D_FF = 2816
AB_PAD = 128
QBLK = BAND_CHUNKS * CHUNK

COL_QKV_A, COL_ZA, COL_QB, COL_KB, COL_VB, COL_GA, COL_GB = 0, 3, 4, 5, 6, 7, 8
N_MAIN = 9 * D_MODEL

VMEM_LIMIT = 56 * 1024 * 1024


def _params(sem):
    return pltpu.CompilerParams(dimension_semantics=sem, vmem_limit_bytes=VMEM_LIMIT)


def _const_spec(shape):
    nd = len(shape)
    return pl.BlockSpec(shape, lambda *_: (0,) * nd, pipeline_mode=pl.Buffered(1))


def _dot(a, b):
    return jnp.dot(a.astype(BF16), b.astype(BF16), preferred_element_type=F32)


def _dot_nt(a, b):
    return lax.dot_general(a.astype(BF16), b.astype(BF16), (((1,), (1,)), ((), ())),
                           preferred_element_type=F32)


def _dot_tn(a, b):
    return lax.dot_general(a.astype(BF16), b.astype(BF16), (((0,), (0,)), ((), ())),
                           preferred_element_type=F32)


def _dot_hi(a, b):
    return jnp.dot(a, b, precision=HI, preferred_element_type=F32)


def _sigmoid(x):
    return 1.0 / (1.0 + jnp.exp(-x))


def _softplus(x):
    return jnp.maximum(x, 0.0) + jnp.log1p(jnp.exp(-jnp.abs(x)))


def _rmsnorm(x, g):
    return x * lax.rsqrt(jnp.mean(x * x, axis=-1, keepdims=True) + EPS) * g


def _inproj_kernel(x_ref, g_ref, w_ref, wab_ref, o_ref, ab_ref, h_ref):
    @pl.when(pl.program_id(1) == 0)
    def _():
        h = _rmsnorm(x_ref[...], g_ref[...]).astype(BF16)
        h_ref[...] = h
        ab_ref[...] = jnp.dot(h, wab_ref[...], preferred_element_type=F32)

    o_ref[...] = jnp.dot(h_ref[...], w_ref[...], preferred_element_type=F32).astype(o_ref.dtype)


def _inproj(x, g, w_main, w_ab, tm, tn):
    t = x.shape[0]
    n = w_main.shape[1]
    return pl.pallas_call(
        _inproj_kernel,
        grid=(t // tm, n // tn),
        in_specs=[
            pl.BlockSpec((tm, D_MODEL), lambda i, j: (i, 0)),
            pl.BlockSpec((1, D_MODEL), lambda i, j: (0, 0)),
            pl.BlockSpec((D_MODEL, tn), lambda i, j: (0, j)),
            pl.BlockSpec((D_MODEL, AB_PAD), lambda i, j: (0, 0)),
        ],
        out_specs=[
            pl.BlockSpec((tm, tn), lambda i, j: (i, j)),
            pl.BlockSpec((tm, AB_PAD), lambda i, j: (i, 0)),
        ],
        out_shape=[
            jax.ShapeDtypeStruct((t, n), BF16),
            jax.ShapeDtypeStruct((t, AB_PAD), F32),
        ],
        scratch_shapes=[pltpu.VMEM((tm, D_MODEL), BF16)],
        compiler_params=_params(("parallel", "arbitrary")),
        name="inproj",
    )(x, g, w_main, w_ab)


def _norm_matmul_kernel(x_ref, g_ref, w_ref, o_ref):
    h = _rmsnorm(x_ref[...], g_ref[...]).astype(BF16)
    o_ref[...] = jnp.dot(h, w_ref[...], preferred_element_type=F32).astype(o_ref.dtype)


def _norm_matmul(x, g, w, tm):
    t = x.shape[0]
    n = w.shape[1]
    return pl.pallas_call(
        _norm_matmul_kernel,
        grid=(t // tm,),
        in_specs=[
            pl.BlockSpec((tm, D_MODEL), lambda i: (i, 0)),
            _const_spec((1, D_MODEL)),
            _const_spec((D_MODEL, n)),
        ],
        out_specs=pl.BlockSpec((tm, n), lambda i: (i, 0)),
        out_shape=jax.ShapeDtypeStruct((t, n), BF16),
        compiler_params=_params(("parallel",)),
        name="mem_kv",
    )(x, g, w)


DELTA_CHUNKS = 2
DELTA_ROWS = DELTA_CHUNKS * CHUNK


def _split_hi_lo(x):
    hi = x.astype(BF16)
    lo = (x - hi.astype(F32)).astype(BF16)
    return hi, lo


def _lhs_split(a, lo_half):
    hi, lo = _split_hi_lo(a)
    s = jnp.where(lo_half, hi, lo)
    return jnp.concatenate([s, s], axis=1)


def _rhs_split(b):
    hi, lo = _split_hi_lo(b)
    return jnp.concatenate([hi, hi, lo, lo], axis=0)


def _mm(a_split, b_split):
    return jnp.dot(a_split, b_split, preferred_element_type=F32)


def _delta_kernel(qkv_ref, za_ref, ab_ref, alt_ref, cw_ref, alog_r_ref, dt_r_ref,
                  alog_c_ref, dt_c_ref, hn_ref, o_ref, s_ref, tail_ref):
    rows = DELTA_ROWS

    @pl.when(pl.program_id(1) == 0)
    def _():
        s_ref[...] = jnp.zeros_like(s_ref)
        tail_ref[...] = jnp.zeros_like(tail_ref)

    xin = qkv_ref[...].astype(F32)
    xx = jnp.concatenate([tail_ref[...], xin], axis=0)
    tail_ref[...] = xin[rows - 8:, :]
    cw = cw_ref[...]
    acc = xin * cw[CONV_K - 1:CONV_K, :]
    for s in range(1, CONV_K):
        shifted = pltpu.roll(xx, s, axis=0)[8:, :]
        acc = acc + shifted * cw[CONV_K - 1 - s:CONV_K - s, :]
    qkv = acc * _sigmoid(acc)

    ab = ab_ref[...]
    g_c = -jnp.exp(alog_r_ref[...]) * _softplus(ab + dt_r_ref[...])
    rr = lax.broadcasted_iota(jnp.int32, (rows, rows), 0)
    cc = lax.broadcasted_iota(jnp.int32, (rows, rows), 1)
    tril_bd = jnp.logical_and(rr >= cc, (rr >> 6) == (cc >> 6)).astype(F32)
    cum_c = _dot_hi(tril_bd, g_c)
    g_r = -jnp.exp(alog_c_ref[...]) * _softplus(alt_ref[0] + dt_c_ref[...])
    rs = lax.broadcasted_iota(jnp.int32, (rows, 2 * rows), 0)
    cs = lax.broadcasted_iota(jnp.int32, (rows, 2 * rows), 1)
    triu_dup = jnp.logical_and((rs >> 6) == (cs >> 7), (rs & 63) <= (cs & 63)).astype(F32)
    cum_r = _dot_hi(g_r, triu_dup)
    beta_c = _sigmoid(ab)
    exp_cum = jnp.exp(cum_c)

    row = lax.broadcasted_iota(jnp.int32, (CHUNK, 2 * CHUNK), 0)
    lane = lax.broadcasted_iota(jnp.int32, (CHUNK, 2 * CHUNK), 1)
    col = lane & (CHUNK - 1)
    lo_half = lane < CHUNK
    incl = row >= col
    strict = row > col
    eye = (row == col).astype(F32)
    same16 = (row >> 4) == (col >> 4)
    same32 = (row >> 5) == (col >> 5)
    off16 = jnp.logical_and(same32, jnp.logical_not(same16))

    chains = [(c, h) for c in range(DELTA_CHUNKS) for h in range(A_HEADS)]

    qn, kn, vv = [], [], []
    for h in range(A_HEADS):
        q = qkv[:, h * A_DK:(h + 1) * A_DK]
        k = qkv[:, A_W + h * A_DK:A_W + (h + 1) * A_DK]
        qn.append(q * lax.rsqrt(jnp.sum(q * q, axis=-1, keepdims=True) + EPS) * (A_DK ** -0.5))
        kn.append(k * lax.rsqrt(jnp.sum(k * k, axis=-1, keepdims=True) + EPS))
        vv.append(qkv[:, 2 * A_W + h * A_DK:2 * A_W + (h + 1) * A_DK])

    lmat, pmat, rhs, kdec, qdec, elast = [], [], [], [], [], []
    for c, h in chains:
        rsl = slice(c * CHUNK, (c + 1) * CHUNK)
        kc = kn[h][rsl]
        qc = qn[h][rsl]
        kq = jnp.concatenate([kc, qc], axis=0).astype(BF16)
        kcat = jnp.concatenate([kc, kc], axis=0).astype(BF16)
        prod = lax.dot_general(kq, kcat, (((1,), (1,)), ((), ())), preferred_element_type=F32)
        cum = cum_c[rsl, h:h + 1]
        diff = cum - cum_r[h:h + 1, c * 2 * CHUNK:(c + 1) * 2 * CHUNK]
        decay = jnp.where(incl, jnp.exp(jnp.where(incl, diff, 0.0)), 0.0)
        beta = beta_c[rsl, A_HEADS + h:A_HEADS + h + 1]
        eg = exp_cum[rsl, h:h + 1]
        g_last = cum_c[(c + 1) * CHUNK - 1:(c + 1) * CHUNK, h:h + 1]
        lmat.append(jnp.where(strict, beta * prod[:CHUNK] * decay, 0.0))
        pmat.append((prod[CHUNK:] * decay)[:, :CHUNK].astype(BF16))
        rhs.append(_rhs_split(jnp.concatenate([beta * vv[h][rsl], (beta * eg) * kc], axis=1)))
        qdec.append(qc * eg)
        kdec.append((kc * jnp.exp(g_last - cum)).astype(BF16))
        elast.append(jnp.exp(g_last))

    d1 = [jnp.where(same16, m, 0.0) for m in lmat]
    d1r = [_rhs_split(m) for m in d1]
    d2 = [_mm(_lhs_split(m, lo_half), r) for m, r in zip(d1, d1r)]
    d2r = [_rhs_split(m) for m in d2]
    d4 = [_mm(_lhs_split(m, lo_half), r) for m, r in zip(d2, d2r)]
    d4r = [_rhs_split(m) for m in d4]
    r0 = [eye - m for m in d1]
    r1 = [m + _mm(_lhs_split(m, lo_half), r) for m, r in zip(r0, d2r)]
    d8 = [_mm(_lhs_split(m, lo_half), r) for m, r in zip(d4, d4r)]
    r2 = [m + _mm(_lhs_split(m, lo_half), r) for m, r in zip(r1, d4r)]
    r3 = [m + _mm(_lhs_split(m, lo_half), _rhs_split(d)) for m, d in zip(r2, d8)]
    t1 = [_mm(_lhs_split(jnp.where(off16, m, 0.0), lo_half), _rhs_split(r))
          for m, r in zip(lmat, r3)]
    r4 = [r - _mm(_lhs_split(r, lo_half), _rhs_split(t)) for r, t in zip(r3, t1)]
    t2 = [_mm(_lhs_split(jnp.where(same32, 0.0, m), lo_half), _rhs_split(r))
          for m, r in zip(lmat, r4)]
    r5 = [r - _mm(_lhs_split(r, lo_half), _rhs_split(t)) for r, t in zip(r4, t2)]
    sol = [_mm(_lhs_split(r, lo_half), b) for r, b in zip(r5, rhs)]

    for c in range(DELTA_CHUNKS):
        rsl = slice(c * CHUNK, (c + 1) * CHUNK)
        for h in range(A_HEADS):
            i = c * A_HEADS + h
            lanes = slice(h * A_DK, (h + 1) * A_DK)
            s_old = s_ref[h]
            wq = jnp.concatenate([sol[i][:, A_DK:], qdec[i]], axis=0)
            wq_s = _dot(wq, s_old)
            u = (sol[i][:, :A_DK] - wq_s[:CHUNK]).astype(BF16)
            o = wq_s[CHUNK:] + jnp.dot(pmat[i], u, preferred_element_type=F32)
            s_ref[h] = s_old * elast[i] + lax.dot_general(
                kdec[i], u, (((0,), (0,)), ((), ())), preferred_element_type=F32)
            on = _rmsnorm(o, hn_ref[...])
            z = za_ref[rsl, lanes].astype(F32)
            o_ref[rsl, lanes] = (on * (z * _sigmoid(z))).astype(o_ref.dtype)


def _delta(p_main, ab, alt, conv_w, alog_r, dt_r, alog_c, dt_c, head_norm, batch, nc):
    t = p_main.shape[0]
    rows = DELTA_ROWS
    return pl.pallas_call(
        _delta_kernel,
        grid=(batch, nc),
        in_specs=[
            pl.BlockSpec((rows, 3 * A_W), lambda b, n: (b * nc + n, COL_QKV_A)),
            pl.BlockSpec((rows, A_W), lambda b, n: (b * nc + n, COL_ZA)),
            pl.BlockSpec((rows, AB_PAD), lambda b, n: (b * nc + n, 0)),
            pl.BlockSpec((1, A_HEADS, rows), lambda b, n: (b * nc + n, 0, 0)),
            _const_spec((CONV_K, 3 * A_W)),
            _const_spec((1, AB_PAD)),
            _const_spec((1, AB_PAD)),
            _const_spec((A_HEADS, 1)),
            _const_spec((A_HEADS, 1)),
            _const_spec((1, A_DK)),
        ],
        out_specs=pl.BlockSpec((rows, A_W), lambda b, n: (b * nc + n, 0)),
        out_shape=jax.ShapeDtypeStruct((t, A_W), BF16),
        scratch_shapes=[
            pltpu.VMEM((A_HEADS, A_DK, A_DK), F32),
            pltpu.VMEM((8, 3 * A_W), F32),
        ],
        compiler_params=_params(("parallel", "arbitrary")),
        name="delta",
    )(p_main, p_main, ab, alt, conv_w, alog_r, dt_r, alog_c, dt_c, head_norm)


def _band_bias_kernel(relt_ref, o_ref):
    rows = o_ref.shape[0] // BAND_LEN
    shape = (rows, BAND_LEN, N_REL_PAD)
    i = lax.broadcasted_iota(jnp.int32, shape, 0) + pl.program_id(0) * rows
    kpos = lax.broadcasted_iota(jnp.int32, shape, 1)
    cand = lax.broadcasted_iota(jnp.int32, shape, 2)
    r = (BAND_CHUNKS - (kpos >> 6)) * CHUNK + i - (kpos & (CHUNK - 1))
    idx = jnp.clip(r, -(CHUNK - 1), MAX_REL) + (CHUNK - 1)
    onehot = (idx == cand).astype(BF16).reshape(rows * BAND_LEN, N_REL_PAD)
    rel = relt_ref[...]
    r1 = rel.astype(BF16)
    rem = rel - r1.astype(F32)
    r2 = rem.astype(BF16)
    r3 = (rem - r2.astype(F32)).astype(BF16)
    acc = jnp.dot(onehot, r1, preferred_element_type=F32)
    acc = acc + jnp.dot(onehot, r2, preferred_element_type=F32)
    acc = acc + jnp.dot(onehot, r3, preferred_element_type=F32)
    o_ref[...] = acc


def _band_bias(rel_bias):
    relt = jnp.pad(rel_bias.T, ((0, N_REL_PAD - N_REL), (0, 0)))
    rows = 8
    out = pl.pallas_call(
        _band_bias_kernel,
        grid=(CHUNK // rows,),
        in_specs=[_const_spec((N_REL_PAD, B_HEADS))],
        out_specs=pl.BlockSpec((rows * BAND_LEN, B_HEADS), lambda g: (g, 0)),
        out_shape=jax.ShapeDtypeStruct((CHUNK * BAND_LEN, B_HEADS), F32),
        compiler_params=_params(("parallel",)),
        name="band_bias",
    )(relt)
    return out.reshape(CHUNK, BAND_LEN, B_HEADS).transpose(2, 0, 1)


def _band_kernel(q_ref, kp_ref, kc_ref, vp_ref, vc_ref, bias_ref, o_ref, kwin, vwin):
    blk = pl.program_id(1)
    kwin[:QBLK, :] = kp_ref[...]
    kwin[QBLK:, :] = kc_ref[...]
    vwin[:QBLK, :] = vp_ref[...]
    vwin[QBLK:, :] = vc_ref[...]
    col_chunk = lax.broadcasted_iota(jnp.int32, (CHUNK, BAND_LEN), 1) >> 6
    scale = B_DH ** -0.5

    def body(a, carry):
        r0 = pl.multiple_of(a * CHUNK, CHUNK)
        valid = (col_chunk + a + BAND_CHUNKS * blk) >= BAND_CHUNKS
        outs = []
        for h in range(B_HEADS):
            lanes = slice(h * B_DH, (h + 1) * B_DH)
            qh = q_ref[pl.ds(r0, CHUNK), lanes]
            kh = kwin[pl.ds(r0, BAND_LEN), lanes]
            vh = vwin[pl.ds(r0, BAND_LEN), lanes]
            sc = _dot_nt(qh, kh) * scale + bias_ref[h]
            sc = jnp.where(valid, sc, -1e30)
            m = jnp.max(sc, axis=-1, keepdims=True)
            p = jnp.exp(sc - m)
            denom = jnp.sum(p, axis=-1, keepdims=True)
            outs.append(_dot(p, vh) / denom)
        o_ref[pl.ds(r0, CHUNK), :] = jnp.concatenate(outs, axis=1).astype(o_ref.dtype)
        return carry

    lax.fori_loop(0, QBLK // CHUNK, body, 0)


def _band(p_main, bias, batch, nblk):
    t = p_main.shape[0]

    def prev(b, i):
        return b * nblk + jnp.maximum(i - 1, 0)

    return pl.pallas_call(
        _band_kernel,
        grid=(batch, nblk),
        in_specs=[
            pl.BlockSpec((QBLK, D_MODEL), lambda b, i: (b * nblk + i, COL_QB)),
            pl.BlockSpec((QBLK, D_MODEL), lambda b, i: (prev(b, i), COL_KB)),
            pl.BlockSpec((QBLK, D_MODEL), lambda b, i: (b * nblk + i, COL_KB)),
            pl.BlockSpec((QBLK, D_MODEL), lambda b, i: (prev(b, i), COL_VB)),
            pl.BlockSpec((QBLK, D_MODEL), lambda b, i: (b * nblk + i, COL_VB)),
            _const_spec((B_HEADS, CHUNK, BAND_LEN)),
        ],
        out_specs=pl.BlockSpec((QBLK, D_MODEL), lambda b, i: (b * nblk + i, 0)),
        out_shape=jax.ShapeDtypeStruct((t, D_MODEL), BF16),
        scratch_shapes=[
            pltpu.VMEM((2 * QBLK, D_MODEL), BF16),
            pltpu.VMEM((2 * QBLK, D_MODEL), BF16),
        ],
        compiler_params=_params(("parallel", "parallel")),
        name="band",
    )(p_main, p_main, p_main, p_main, p_main, bias)


def _merge_kernel(oa_ref, ob_ref, ga_ref, gb_ref, x_ref, wa_ref, wb_ref, wo_ref, o_ref):
    ya = jnp.dot(oa_ref[...], wa_ref[...], preferred_element_type=F32)
    yb = jnp.dot(ob_ref[...], wb_ref[...], preferred_element_type=F32)
    y = _sigmoid(ga_ref[...].astype(F32)) * ya + _sigmoid(gb_ref[...].astype(F32)) * yb
    o_ref[...] = x_ref[...] + jnp.dot(y.astype(BF16), wo_ref[...], preferred_element_type=F32)


def _merge(oa, ob, p_main, x, wa, wb, wo, tm):
    t = x.shape[0]
    row = lambda i: (i, 0)
    return pl.pallas_call(
        _merge_kernel,
        grid=(t // tm,),
        in_specs=[
            pl.BlockSpec((tm, D_MODEL), row),
            pl.BlockSpec((tm, D_MODEL), row),
            pl.BlockSpec((tm, D_MODEL), lambda i: (i, COL_GA)),
            pl.BlockSpec((tm, D_MODEL), lambda i: (i, COL_GB)),
            pl.BlockSpec((tm, D_MODEL), row),
            _const_spec((D_MODEL, D_MODEL)),
            _const_spec((D_MODEL, D_MODEL)),
            _const_spec((D_MODEL, D_MODEL)),
        ],
        out_specs=pl.BlockSpec((tm, D_MODEL), row),
        out_shape=jax.ShapeDtypeStruct((t, D_MODEL), F32),
        compiler_params=_params(("parallel",)),
        name="merge",
    )(oa, ob, p_main, p_main, x, wa, wb, wo)


def _xattn_kernel(x_ref, g_ref, wq_ref, kv_ref, wo_ref, o_ref):
    x = x_ref[...]
    h = _rmsnorm(x, g_ref[...]).astype(BF16)
    q = jnp.dot(h, wq_ref[...], preferred_element_type=F32).astype(BF16)
    scale = M_DH ** -0.5
    outs = []
    for hh in range(M_HEADS):
        kh = kv_ref[:, hh * M_DH:(hh + 1) * M_DH]
        vh = kv_ref[:, D_MODEL + hh * M_DH:D_MODEL + (hh + 1) * M_DH]
        sc = _dot_nt(q[:, hh * M_DH:(hh + 1) * M_DH], kh) * scale
        m = jnp.max(sc, axis=-1, keepdims=True)
        p = jnp.exp(sc - m)
        denom = jnp.sum(p, axis=-1, keepdims=True)
        outs.append((_dot(p, vh) / denom).astype(BF16))
    o = jnp.concatenate(outs, axis=1)
    o_ref[...] = x + jnp.dot(o, wo_ref[...], preferred_element_type=F32)


def _xattn(x, g, wq, kv, wo, tm, seq, mem_len):
    t = x.shape[0]
    per_batch = seq // tm
    return pl.pallas_call(
        _xattn_kernel,
        grid=(t // tm,),
        in_specs=[
            pl.BlockSpec((tm, D_MODEL), lambda i: (i, 0)),
            _const_spec((1, D_MODEL)),
            _const_spec((D_MODEL, D_MODEL)),
            pl.BlockSpec((mem_len, 2 * D_MODEL), lambda i: (i // per_batch, 0)),
            _const_spec((D_MODEL, D_MODEL)),
        ],
        out_specs=pl.BlockSpec((tm, D_MODEL), lambda i: (i, 0)),
        out_shape=jax.ShapeDtypeStruct((t, D_MODEL), F32),
        compiler_params=_params(("parallel",)),
        name="xattn",
    )(x, g, wq, kv, wo)


FF_CHUNK = 256


def _ffn_kernel(x_ref, g_ref, wgu_ref, wd_ref, gf_ref, o_ref, act_ref, *, final_norm):
    x = x_ref[...]
    h = _rmsnorm(x, g_ref[...]).astype(BF16)
    for c in range(D_FF // FF_CHUNK):
        lo = c * FF_CHUNK
        gate = jnp.dot(h, wgu_ref[:, lo:lo + FF_CHUNK], preferred_element_type=F32)
        up = jnp.dot(h, wgu_ref[:, D_FF + lo:D_FF + lo + FF_CHUNK], preferred_element_type=F32)
        act_ref[:, lo:lo + FF_CHUNK] = (gate * _sigmoid(gate) * up).astype(BF16)
    y = x + jnp.dot(act_ref[...], wd_ref[...], preferred_element_type=F32)
    if final_norm:
        y = _rmsnorm(y, gf_ref[...])
    o_ref[...] = y


def _ffn(x, g, wgu, wd, gf, tm, final_norm):
    t = x.shape[0]
    return pl.pallas_call(
        functools.partial(_ffn_kernel, final_norm=final_norm),
        grid=(t // tm,),
        in_specs=[
            pl.BlockSpec((tm, D_MODEL), lambda i: (i, 0)),
            _const_spec((1, D_MODEL)),
            _const_spec((D_MODEL, 2 * D_FF)),
            _const_spec((D_FF, D_MODEL)),
            _const_spec((1, D_MODEL)),
        ],
        out_specs=pl.BlockSpec((tm, D_MODEL), lambda i: (i, 0)),
        out_shape=jax.ShapeDtypeStruct((t, D_MODEL), F32),
        scratch_shapes=[pltpu.VMEM((tm, D_FF), BF16)],
        compiler_params=_params(("parallel",)),
        name="ffn",
    )(x, g, wgu, wd, gf)


def kernel(x, mem, norm_mix, w_in, conv_w, a_log, dt_bias, head_norm, w_a_out, w_b_out,
           rel_bias, w_o, norm_xattn, norm_mem, w_mq, w_mkv, w_mo, norm_ffn, w_gate_up,
           w_down, norm_final):
    batch, seq, d = x.shape
    mem_len = mem.shape[1]
    depth = w_in.shape[0]
    t = batch * seq
    nsteps = seq // DELTA_ROWS
    nblk = seq // QBLK
    tm_in = min(1024, t)
    tm = min(512, seq)

    xf = x.reshape(t, d)
    memf = mem.reshape(batch * mem_len, d)
    bias = _band_bias(rel_bias)
    ab_lo = 4 * A_W
    ab_hi = ab_lo + 2 * A_HEADS

    def pad_lanes(v):
        return jnp.pad(v.reshape(1, -1), ((0, 0), (0, AB_PAD - v.shape[-1])))

    for l in range(depth):
        w_l = w_in[l]
        w_main = jnp.concatenate([w_l[:, :ab_lo], w_l[:, ab_hi:]], axis=1).astype(BF16)
        w_ab = jnp.pad(w_l[:, ab_lo:ab_hi], ((0, 0), (0, AB_PAD - 2 * A_HEADS))).astype(BF16)
        p_main, ab = _inproj(xf, norm_mix[l].reshape(1, d), w_main, w_ab, tm_in, D_MODEL)

        alt = ab[:, :A_HEADS].reshape(batch * nsteps, DELTA_ROWS, A_HEADS).transpose(0, 2, 1)
        oa = _delta(p_main, ab, alt, conv_w[l], pad_lanes(a_log[l]), pad_lanes(dt_bias[l]),
                    a_log[l].reshape(A_HEADS, 1), dt_bias[l].reshape(A_HEADS, 1),
                    head_norm[l].reshape(1, A_DK), batch, nsteps)
        ob = _band(p_main, bias, batch, nblk)
        x1 = _merge(oa, ob, p_main, xf, w_a_out[l].astype(BF16), w_b_out[l].astype(BF16),
                    w_o[l].astype(BF16), tm)

        kv = _norm_matmul(memf, norm_mem[l].reshape(1, d), w_mkv[l].astype(BF16),
                          min(512, batch * mem_len))
        x2 = _xattn(x1, norm_xattn[l].reshape(1, d), w_mq[l].astype(BF16), kv,
                    w_mo[l].astype(BF16), tm, seq, mem_len)

        xf = _ffn(x2, norm_ffn[l].reshape(1, d), w_gate_up[l].astype(BF16),
                  w_down[l].astype(BF16), norm_final.reshape(1, d), tm,
                  final_norm=(l == depth - 1))
    return xf.reshape(batch, seq, d)
```

```python
import functools

import jax
import jax.numpy as jnp
from jax import lax
from jax.experimental import pallas as pl
from jax.experimental.pallas import tpu as pltpu

F32 = jnp.float32
BF16 = jnp.bfloat16
HI = lax.Precision.HIGHEST

D_MODEL = 1024
CHUNK = 64
EPS = 1e-6
A_HEADS = 8
A_DK = 128
A_W = A_HEADS * A_DK
CONV_K = 4
B_HEADS = 16
B_DH = 64
BAND_CHUNKS = 8
BAND_LEN = (BAND_CHUNKS + 1) * CHUNK
MAX_REL = 256
N_REL = (CHUNK - 1) + MAX_REL + 1
N_REL_PAD = 384
M_HEADS = 4
M_DH = D_MODEL // M_HEADS
D_FF = 2816
AB_PAD = 128
QBLK = BAND_CHUNKS * CHUNK

COL_QKV_A, COL_ZA, COL_QB, COL_KB, COL_VB, COL_GA, COL_GB = 0, 3, 4, 5, 6, 7, 8
N_MAIN = 9 * D_MODEL

VMEM_LIMIT = 56 * 1024 * 1024


def _params(sem):
    return pltpu.CompilerParams(dimension_semantics=sem, vmem_limit_bytes=VMEM_LIMIT)


def _const_spec(shape):
    nd = len(shape)
    return pl.BlockSpec(shape, lambda *_: (0,) * nd, pipeline_mode=pl.Buffered(1))


def _dot(a, b):
    return jnp.dot(a.astype(BF16), b.astype(BF16), preferred_element_type=F32)


def _dot_nt(a, b):
    return lax.dot_general(a.astype(BF16), b.astype(BF16), (((1,), (1,)), ((), ())),
                           preferred_element_type=F32)


def _dot_tn(a, b):
    return lax.dot_general(a.astype(BF16), b.astype(BF16), (((0,), (0,)), ((), ())),
                           preferred_element_type=F32)


def _dot_hi(a, b):
    return jnp.dot(a, b, precision=HI, preferred_element_type=F32)


def _sigmoid(x):
    return 1.0 / (1.0 + jnp.exp(-x))


def _softplus(x):
    return jnp.maximum(x, 0.0) + jnp.log1p(jnp.exp(-jnp.abs(x)))


def _rmsnorm(x, g):
    return x * lax.rsqrt(jnp.mean(x * x, axis=-1, keepdims=True) + EPS) * g


def _inproj_kernel(x_ref, g_ref, w_ref, wab_ref, o_ref, ab_ref, h_ref):
    @pl.when(pl.program_id(1) == 0)
    def _():
        h = _rmsnorm(x_ref[...], g_ref[...]).astype(BF16)
        h_ref[...] = h
        ab_ref[...] = jnp.dot(h, wab_ref[...], preferred_element_type=F32)

    o_ref[...] = jnp.dot(h_ref[...], w_ref[...], preferred_element_type=F32).astype(o_ref.dtype)


def _inproj(x, g, w_main, w_ab, tm, tn):
    t = x.shape[0]
    n = w_main.shape[1]
    return pl.pallas_call(
        _inproj_kernel,
        grid=(t // tm, n // tn),
        in_specs=[
            pl.BlockSpec((tm, D_MODEL), lambda i, j: (i, 0)),
            pl.BlockSpec((1, D_MODEL), lambda i, j: (0, 0)),
            pl.BlockSpec((D_MODEL, tn), lambda i, j: (0, j)),
            pl.BlockSpec((D_MODEL, AB_PAD), lambda i, j: (0, 0)),
        ],
        out_specs=[
            pl.BlockSpec((tm, tn), lambda i, j: (i, j)),
            pl.BlockSpec((tm, AB_PAD), lambda i, j: (i, 0)),
        ],
        out_shape=[
            jax.ShapeDtypeStruct((t, n), BF16),
            jax.ShapeDtypeStruct((t, AB_PAD), F32),
        ],
        scratch_shapes=[pltpu.VMEM((tm, D_MODEL), BF16)],
        compiler_params=_params(("parallel", "arbitrary")),
        name="inproj",
    )(x, g, w_main, w_ab)


def _norm_matmul_kernel(x_ref, g_ref, w_ref, o_ref):
    h = _rmsnorm(x_ref[...], g_ref[...]).astype(BF16)
    o_ref[...] = jnp.dot(h, w_ref[...], preferred_element_type=F32).astype(o_ref.dtype)


def _norm_matmul(x, g, w, tm):
    t = x.shape[0]
    n = w.shape[1]
    return pl.pallas_call(
        _norm_matmul_kernel,
        grid=(t // tm,),
        in_specs=[
            pl.BlockSpec((tm, D_MODEL), lambda i: (i, 0)),
            _const_spec((1, D_MODEL)),
            _const_spec((D_MODEL, n)),
        ],
        out_specs=pl.BlockSpec((tm, n), lambda i: (i, 0)),
        out_shape=jax.ShapeDtypeStruct((t, n), BF16),
        compiler_params=_params(("parallel",)),
        name="mem_kv",
    )(x, g, w)


DELTA_CHUNKS = 2
DELTA_ROWS = DELTA_CHUNKS * CHUNK


def _split_hi_lo(x):
    hi = x.astype(BF16)
    lo = (x - hi.astype(F32)).astype(BF16)
    return hi, lo


def _lhs_split(a, lo_half):
    hi, lo = _split_hi_lo(a)
    s = jnp.where(lo_half, hi, lo)
    return jnp.concatenate([s, s], axis=1)


def _rhs_split(b):
    hi, lo = _split_hi_lo(b)
    return jnp.concatenate([hi, hi, lo, lo], axis=0)


def _mm(a_split, b_split):
    return jnp.dot(a_split, b_split, preferred_element_type=F32)


def _delta_kernel(qkv_ref, za_ref, ab_ref, alt_ref, cw_ref, alog_r_ref, dt_r_ref,
                  alog_c_ref, dt_c_ref, hn_ref, o_ref, s_ref, tail_ref):
    rows = DELTA_ROWS

    @pl.when(pl.program_id(1) == 0)
    def _():
        s_ref[...] = jnp.zeros_like(s_ref)
        tail_ref[...] = jnp.zeros_like(tail_ref)

    xin = qkv_ref[...].astype(F32)
    xx = jnp.concatenate([tail_ref[...], xin], axis=0)
    tail_ref[...] = xin[rows - 8:, :]
    cw = cw_ref[...]
    acc = xin * cw[CONV_K - 1:CONV_K, :]
    for s in range(1, CONV_K):
        shifted = pltpu.roll(xx, s, axis=0)[8:, :]
        acc = acc + shifted * cw[CONV_K - 1 - s:CONV_K - s, :]
    qkv = acc * _sigmoid(acc)

    ab = ab_ref[...]
    g_c = -jnp.exp(alog_r_ref[...]) * _softplus(ab + dt_r_ref[...])
    rr = lax.broadcasted_iota(jnp.int32, (rows, rows), 0)
    cc = lax.broadcasted_iota(jnp.int32, (rows, rows), 1)
    tril_bd = jnp.logical_and(rr >= cc, (rr >> 6) == (cc >> 6)).astype(F32)
    cum_c = _dot_hi(tril_bd, g_c)
    g_r = -jnp.exp(alog_c_ref[...]) * _softplus(alt_ref[0] + dt_c_ref[...])
    rs = lax.broadcasted_iota(jnp.int32, (rows, 2 * rows), 0)
    cs = lax.broadcasted_iota(jnp.int32, (rows, 2 * rows), 1)
    triu_dup = jnp.logical_and((rs >> 6) == (cs >> 7), (rs & 63) <= (cs & 63)).astype(F32)
    cum_r = _dot_hi(g_r, triu_dup)
    beta_c = _sigmoid(ab)
    exp_cum = jnp.exp(cum_c)

    row = lax.broadcasted_iota(jnp.int32, (CHUNK, 2 * CHUNK), 0)
    lane = lax.broadcasted_iota(jnp.int32, (CHUNK, 2 * CHUNK), 1)
    col = lane & (CHUNK - 1)
    lo_half = lane < CHUNK
    incl = row >= col
    strict = row > col
    eye = (row == col).astype(F32)
    same16 = (row >> 4) == (col >> 4)
    same32 = (row >> 5) == (col >> 5)
    off16 = jnp.logical_and(same32, jnp.logical_not(same16))

    chains = [(c, h) for c in range(DELTA_CHUNKS) for h in range(A_HEADS)]

    qn, kn, vv = [], [], []
    for h in range(A_HEADS):
        q = qkv[:, h * A_DK:(h + 1) * A_DK]
        k = qkv[:, A_W + h * A_DK:A_W + (h + 1) * A_DK]
        qn.append(q * lax.rsqrt(jnp.sum(q * q, axis=-1, keepdims=True) + EPS) * (A_DK ** -0.5))
        kn.append(k * lax.rsqrt(jnp.sum(k * k, axis=-1, keepdims=True) + EPS))
        vv.append(qkv[:, 2 * A_W + h * A_DK:2 * A_W + (h + 1) * A_DK])

    lmat, pmat, rhs, kdec, qdec, elast = [], [], [], [], [], []
    for c, h in chains:
        rsl = slice(c * CHUNK, (c + 1) * CHUNK)
        kc = kn[h][rsl]
        qc = qn[h][rsl]
        kq = jnp.concatenate([kc, qc], axis=0).astype(BF16)
        kcat = jnp.concatenate([kc, kc], axis=0).astype(BF16)
        prod = lax.dot_general(kq, kcat, (((1,), (1,)), ((), ())), preferred_element_type=F32)
        cum = cum_c[rsl, h:h + 1]
        diff = cum - cum_r[h:h + 1, c * 2 * CHUNK:(c + 1) * 2 * CHUNK]
        decay = jnp.where(incl, jnp.exp(jnp.where(incl, diff, 0.0)), 0.0)
        beta = beta_c[rsl, A_HEADS + h:A_HEADS + h + 1]
        eg = exp_cum[rsl, h:h + 1]
        g_last = cum_c[(c + 1) * CHUNK - 1:(c + 1) * CHUNK, h:h + 1]
        lmat.append(jnp.where(strict, beta * prod[:CHUNK] * decay, 0.0))
        pmat.append((prod[CHUNK:] * decay)[:, :CHUNK].astype(BF16))
        rhs.append(_rhs_split(jnp.concatenate([beta * vv[h][rsl], (beta * eg) * kc], axis=1)))
        qdec.append(qc * eg)
        kdec.append((kc * jnp.exp(g_last - cum)).astype(BF16))
        elast.append(jnp.exp(g_last))

    d1 = [jnp.where(same16, m, 0.0) for m in lmat]
    d1r = [_rhs_split(m) for m in d1]
    d2 = [_mm(_lhs_split(m, lo_half), r) for m, r in zip(d1, d1r)]
    d2r = [_rhs_split(m) for m in d2]
    d4 = [_mm(_lhs_split(m, lo_half), r) for m, r in zip(d2, d2r)]
    d4r = [_rhs_split(m) for m in d4]
    r0 = [eye - m for m in d1]
    r1 = [m + _mm(_lhs_split(m, lo_half), r) for m, r in zip(r0, d2r)]
    d8 = [_mm(_lhs_split(m, lo_half), r) for m, r in zip(d4, d4r)]
    r2 = [m + _mm(_lhs_split(m, lo_half), r) for m, r in zip(r1, d4r)]
    r3 = [m + _mm(_lhs_split(m, lo_half), _rhs_split(d)) for m, d in zip(r2, d8)]
    t1 = [_mm(_lhs_split(jnp.where(off16, m, 0.0), lo_half), _rhs_split(r))
          for m, r in zip(lmat, r3)]
    r4 = [r - _mm(_lhs_split(r, lo_half), _rhs_split(t)) for r, t in zip(r3, t1)]
    t2 = [_mm(_lhs_split(jnp.where(same32, 0.0, m), lo_half), _rhs_split(r))
          for m, r in zip(lmat, r4)]
    r5 = [r - _mm(_lhs_split(r, lo_half), _rhs_split(t)) for r, t in zip(r4, t2)]
    sol = [_mm(_lhs_split(r, lo_half), b) for r, b in zip(r5, rhs)]

    for c in range(DELTA_CHUNKS):
        rsl = slice(c * CHUNK, (c + 1) * CHUNK)
        for h in range(A_HEADS):
            i = c * A_HEADS + h
            lanes = slice(h * A_DK, (h + 1) * A_DK)
            s_old = s_ref[h]
            wq = jnp.concatenate([sol[i][:, A_DK:], qdec[i]], axis=0)
            wq_s = _dot(wq, s_old)
            u = (sol[i][:, :A_DK] - wq_s[:CHUNK]).astype(BF16)
            o = wq_s[CHUNK:] + jnp.dot(pmat[i], u, preferred_element_type=F32)
            s_ref[h] = s_old * elast[i] + lax.dot_general(
                kdec[i], u, (((0,), (0,)), ((), ())), preferred_element_type=F32)
            on = _rmsnorm(o, hn_ref[...])
            z = za_ref[rsl, lanes].astype(F32)
            o_ref[rsl, lanes] = (on * (z * _sigmoid(z))).astype(o_ref.dtype)


def _delta(p_main, ab, alt, conv_w, alog_r, dt_r, alog_c, dt_c, head_norm, batch, nc):
    t = p_main.shape[0]
    rows = DELTA_ROWS
    return pl.pallas_call(
        _delta_kernel,
        grid=(batch, nc),
        in_specs=[
            pl.BlockSpec((rows, 3 * A_W), lambda b, n: (b * nc + n, COL_QKV_A)),
            pl.BlockSpec((rows, A_W), lambda b, n: (b * nc + n, COL_ZA)),
            pl.BlockSpec((rows, AB_PAD), lambda b, n: (b * nc + n, 0)),
            pl.BlockSpec((1, A_HEADS, rows), lambda b, n: (b * nc + n, 0, 0)),
            _const_spec((CONV_K, 3 * A_W)),
            _const_spec((1, AB_PAD)),
            _const_spec((1, AB_PAD)),
            _const_spec((A_HEADS, 1)),
            _const_spec((A_HEADS, 1)),
            _const_spec((1, A_DK)),
        ],
        out_specs=pl.BlockSpec((rows, A_W), lambda b, n: (b * nc + n, 0)),
        out_shape=jax.ShapeDtypeStruct((t, A_W), BF16),
        scratch_shapes=[
            pltpu.VMEM((A_HEADS, A_DK, A_DK), F32),
            pltpu.VMEM((8, 3 * A_W), F32),
        ],
        compiler_params=_params(("parallel", "arbitrary")),
        name="delta",
    )(p_main, p_main, ab, alt, conv_w, alog_r, dt_r, alog_c, dt_c, head_norm)


def _band_bias_kernel(relt_ref, o_ref):
    rows = o_ref.shape[0] // BAND_LEN
    shape = (rows, BAND_LEN, N_REL_PAD)
    i = lax.broadcasted_iota(jnp.int32, shape, 0) + pl.program_id(0) * rows
    kpos = lax.broadcasted_iota(jnp.int32, shape, 1)
    cand = lax.broadcasted_iota(jnp.int32, shape, 2)
    r = (BAND_CHUNKS - (kpos >> 6)) * CHUNK + i - (kpos & (CHUNK - 1))
    idx = jnp.clip(r, -(CHUNK - 1), MAX_REL) + (CHUNK - 1)
    onehot = (idx == cand).astype(BF16).reshape(rows * BAND_LEN, N_REL_PAD)
    rel = relt_ref[...]
    r1 = rel.astype(BF16)
    rem = rel - r1.astype(F32)
    r2 = rem.astype(BF16)
    r3 = (rem - r2.astype(F32)).astype(BF16)
    acc = jnp.dot(onehot, r1, preferred_element_type=F32)
    acc = acc + jnp.dot(onehot, r2, preferred_element_type=F32)
    acc = acc + jnp.dot(onehot, r3, preferred_element_type=F32)
    o_ref[...] = acc


def _band_bias(rel_bias):
    relt = jnp.pad(rel_bias.T, ((0, N_REL_PAD - N_REL), (0, 0)))
    rows = 8
    out = pl.pallas_call(
        _band_bias_kernel,
        grid=(CHUNK // rows,),
        in_specs=[_const_spec((N_REL_PAD, B_HEADS))],
        out_specs=pl.BlockSpec((rows * BAND_LEN, B_HEADS), lambda g: (g, 0)),
        out_shape=jax.ShapeDtypeStruct((CHUNK * BAND_LEN, B_HEADS), F32),
        compiler_params=_params(("parallel",)),
        name="band_bias",
    )(relt)
    return out.reshape(CHUNK, BAND_LEN, B_HEADS).transpose(2, 0, 1)


def _band_kernel(q_ref, kp_ref, kc_ref, vp_ref, vc_ref, bias_ref, o_ref, kwin, vwin):
    blk = pl.program_id(1)
    kwin[:QBLK, :] = kp_ref[...]
    kwin[QBLK:, :] = kc_ref[...]
    vwin[:QBLK, :] = vp_ref[...]
    vwin[QBLK:, :] = vc_ref[...]
    col_chunk = lax.broadcasted_iota(jnp.int32, (2 * CHUNK, BAND_LEN), 1) >> 6
    lane = lax.broadcasted_iota(jnp.int32, (CHUNK, 2 * B_DH), 1)
    even = lane < B_DH
    scale = B_DH ** -0.5
    pairs = B_HEADS // 2
    group = 4

    def body(a, carry):
        r0 = pl.multiple_of(a * CHUNK, CHUNK)
        valid = (col_chunk + a + BAND_CHUNKS * blk) >= BAND_CHUNKS
        for g0 in range(0, pairs, group):
            sc = []
            for pr in range(g0, g0 + group):
                lanes = slice(pr * 2 * B_DH, (pr + 1) * 2 * B_DH)
                qp = q_ref[pl.ds(r0, CHUNK), lanes] * scale
                q2 = jnp.concatenate([jnp.where(even, qp, 0), jnp.where(even, 0, qp)], axis=0)
                kp = kwin[pl.ds(r0, BAND_LEN), lanes]
                bias2 = jnp.concatenate([bias_ref[2 * pr], bias_ref[2 * pr + 1]], axis=0)
                s = lax.dot_general(q2.astype(BF16), kp, (((1,), (1,)), ((), ())),
                                    preferred_element_type=F32) + bias2
                sc.append(jnp.where(valid, s, -1e30))
            m = [jnp.max(s, axis=-1, keepdims=True) for s in sc]
            p = [jnp.exp(s - mm) for s, mm in zip(sc, m)]
            denom = [jnp.sum(pp, axis=-1, keepdims=True) for pp in p]
            for i, pr in enumerate(range(g0, g0 + group)):
                lanes = slice(pr * 2 * B_DH, (pr + 1) * 2 * B_DH)
                vp = vwin[pl.ds(r0, BAND_LEN), lanes]
                o2 = jnp.dot(p[i].astype(BF16), vp, preferred_element_type=F32) / denom[i]
                o_ref[pl.ds(r0, CHUNK), lanes] = jnp.where(
                    even, o2[:CHUNK], o2[CHUNK:]).astype(o_ref.dtype)
        return carry

    lax.fori_loop(0, QBLK // CHUNK, body, 0)


def _band(p_main, bias, batch, nblk):
    t = p_main.shape[0]

    def prev(b, i):
        return b * nblk + jnp.maximum(i - 1, 0)

    return pl.pallas_call(
        _band_kernel,
        grid=(batch, nblk),
        in_specs=[
            pl.BlockSpec((QBLK, D_MODEL), lambda b, i: (b * nblk + i, COL_QB)),
            pl.BlockSpec((QBLK, D_MODEL), lambda b, i: (prev(b, i), COL_KB)),
            pl.BlockSpec((QBLK, D_MODEL), lambda b, i: (b * nblk + i, COL_KB)),
            pl.BlockSpec((QBLK, D_MODEL), lambda b, i: (prev(b, i), COL_VB)),
            pl.BlockSpec((QBLK, D_MODEL), lambda b, i: (b * nblk + i, COL_VB)),
            _const_spec((B_HEADS, CHUNK, BAND_LEN)),
        ],
        out_specs=pl.BlockSpec((QBLK, D_MODEL), lambda b, i: (b * nblk + i, 0)),
        out_shape=jax.ShapeDtypeStruct((t, D_MODEL), BF16),
        scratch_shapes=[
            pltpu.VMEM((2 * QBLK, D_MODEL), BF16),
            pltpu.VMEM((2 * QBLK, D_MODEL), BF16),
        ],
        compiler_params=_params(("parallel", "parallel")),
        name="band",
    )(p_main, p_main, p_main, p_main, p_main, bias)


def _merge_kernel(oa_ref, ob_ref, ga_ref, gb_ref, x_ref, wa_ref, wb_ref, wo_ref, o_ref):
    ya = jnp.dot(oa_ref[...], wa_ref[...], preferred_element_type=F32)
    yb = jnp.dot(ob_ref[...], wb_ref[...], preferred_element_type=F32)
    y = _sigmoid(ga_ref[...].astype(F32)) * ya + _sigmoid(gb_ref[...].astype(F32)) * yb
    o_ref[...] = x_ref[...] + jnp.dot(y.astype(BF16), wo_ref[...], preferred_element_type=F32)


def _merge(oa, ob, p_main, x, wa, wb, wo, tm):
    t = x.shape[0]
    row = lambda i: (i, 0)
    return pl.pallas_call(
        _merge_kernel,
        grid=(t // tm,),
        in_specs=[
            pl.BlockSpec((tm, D_MODEL), row),
            pl.BlockSpec((tm, D_MODEL), row),
            pl.BlockSpec((tm, D_MODEL), lambda i: (i, COL_GA)),
            pl.BlockSpec((tm, D_MODEL), lambda i: (i, COL_GB)),
            pl.BlockSpec((tm, D_MODEL), row),
            _const_spec((D_MODEL, D_MODEL)),
            _const_spec((D_MODEL, D_MODEL)),
            _const_spec((D_MODEL, D_MODEL)),
        ],
        out_specs=pl.BlockSpec((tm, D_MODEL), row),
        out_shape=jax.ShapeDtypeStruct((t, D_MODEL), F32),
        compiler_params=_params(("parallel",)),
        name="merge",
    )(oa, ob, p_main, p_main, x, wa, wb, wo)


def _xattn_kernel(x_ref, g_ref, wq_ref, kv_ref, wo_ref, o_ref):
    x = x_ref[...]
    h = _rmsnorm(x, g_ref[...]).astype(BF16)
    q = jnp.dot(h, wq_ref[...], preferred_element_type=F32).astype(BF16)
    scale = M_DH ** -0.5
    outs = []
    for hh in range(M_HEADS):
        kh = kv_ref[:, hh * M_DH:(hh + 1) * M_DH]
        vh = kv_ref[:, D_MODEL + hh * M_DH:D_MODEL + (hh + 1) * M_DH]
        sc = _dot_nt(q[:, hh * M_DH:(hh + 1) * M_DH], kh) * scale
        m = jnp.max(sc, axis=-1, keepdims=True)
        p = jnp.exp(sc - m)
        denom = jnp.sum(p, axis=-1, keepdims=True)
        outs.append((_dot(p, vh) / denom).astype(BF16))
    o = jnp.concatenate(outs, axis=1)
    o_ref[...] = x + jnp.dot(o, wo_ref[...], preferred_element_type=F32)


def _xattn(x, g, wq, kv, wo, tm, seq, mem_len):
    t = x.shape[0]
    per_batch = seq // tm
    return pl.pallas_call(
        _xattn_kernel,
        grid=(t // tm,),
        in_specs=[
            pl.BlockSpec((tm, D_MODEL), lambda i: (i, 0)),
            _const_spec((1, D_MODEL)),
            _const_spec((D_MODEL, D_MODEL)),
            pl.BlockSpec((mem_len, 2 * D_MODEL), lambda i: (i // per_batch, 0)),
            _const_spec((D_MODEL, D_MODEL)),
        ],
        out_specs=pl.BlockSpec((tm, D_MODEL), lambda i: (i, 0)),
        out_shape=jax.ShapeDtypeStruct((t, D_MODEL), F32),
        compiler_params=_params(("parallel",)),
        name="xattn",
    )(x, g, wq, kv, wo)


FF_CHUNK = 256


def _ffn_kernel(x_ref, g_ref, wgu_ref, wd_ref, gf_ref, o_ref, act_ref, *, final_norm):
    x = x_ref[...]
    h = _rmsnorm(x, g_ref[...]).astype(BF16)
    for c in range(D_FF // FF_CHUNK):
        lo = c * FF_CHUNK
        gate = jnp.dot(h, wgu_ref[:, lo:lo + FF_CHUNK], preferred_element_type=F32)
        up = jnp.dot(h, wgu_ref[:, D_FF + lo:D_FF + lo + FF_CHUNK], preferred_element_type=F32)
        act_ref[:, lo:lo + FF_CHUNK] = (gate * _sigmoid(gate) * up).astype(BF16)
    y = x + jnp.dot(act_ref[...], wd_ref[...], preferred_element_type=F32)
    if final_norm:
        y = _rmsnorm(y, gf_ref[...])
    o_ref[...] = y


def _ffn(x, g, wgu, wd, gf, tm, final_norm):
    t = x.shape[0]
    return pl.pallas_call(
        functools.partial(_ffn_kernel, final_norm=final_norm),
        grid=(t // tm,),
        in_specs=[
            pl.BlockSpec((tm, D_MODEL), lambda i: (i, 0)),
            _const_spec((1, D_MODEL)),
            _const_spec((D_MODEL, 2 * D_FF)),
            _const_spec((D_FF, D_MODEL)),
            _const_spec((1, D_MODEL)),
        ],
        out_specs=pl.BlockSpec((tm, D_MODEL), lambda i: (i, 0)),
        out_shape=jax.ShapeDtypeStruct((t, D_MODEL), F32),
        scratch_shapes=[pltpu.VMEM((tm, D_FF), BF16)],
        compiler_params=_params(("parallel",)),
        name="ffn",
    )(x, g, wgu, wd, gf)


def kernel(x, mem, norm_mix, w_in, conv_w, a_log, dt_bias, head_norm, w_a_out, w_b_out,
           rel_bias, w_o, norm_xattn, norm_mem, w_mq, w_mkv, w_mo, norm_ffn, w_gate_up,
           w_down, norm_final):
    batch, seq, d = x.shape
    mem_len = mem.shape[1]
    depth = w_in.shape[0]
    t = batch * seq
    nsteps = seq // DELTA_ROWS
    nblk = seq // QBLK
    tm_in = min(1024, t)
    tm = min(512, seq)

    xf = x.reshape(t, d)
    memf = mem.reshape(batch * mem_len, d)
    bias = _band_bias(rel_bias)
    ab_lo = 4 * A_W
    ab_hi = ab_lo + 2 * A_HEADS

    def pad_lanes(v):
        return jnp.pad(v.reshape(1, -1), ((0, 0), (0, AB_PAD - v.shape[-1])))

    for l in range(depth):
        w_l = w_in[l]
        w_main = jnp.concatenate([w_l[:, :ab_lo], w_l[:, ab_hi:]], axis=1).astype(BF16)
        w_ab = jnp.pad(w_l[:, ab_lo:ab_hi], ((0, 0), (0, AB_PAD - 2 * A_HEADS))).astype(BF16)
        p_main, ab = _inproj(xf, norm_mix[l].reshape(1, d), w_main, w_ab, tm_in, D_MODEL)

        alt = ab[:, :A_HEADS].reshape(batch * nsteps, DELTA_ROWS, A_HEADS).transpose(0, 2, 1)
        oa = _delta(p_main, ab, alt, conv_w[l], pad_lanes(a_log[l]), pad_lanes(dt_bias[l]),
                    a_log[l].reshape(A_HEADS, 1), dt_bias[l].reshape(A_HEADS, 1),
                    head_norm[l].reshape(1, A_DK), batch, nsteps)
        ob = _band(p_main, bias, batch, nblk)
        x1 = _merge(oa, ob, p_main, xf, w_a_out[l].astype(BF16), w_b_out[l].astype(BF16),
                    w_o[l].astype(BF16), tm)

        kv = _norm_matmul(memf, norm_mem[l].reshape(1, d), w_mkv[l].astype(BF16),
                          min(512, batch * mem_len))
        x2 = _xattn(x1, norm_xattn[l].reshape(1, d), w_mq[l].astype(BF16), kv,
                    w_mo[l].astype(BF16), tm, seq, mem_len)

        xf = _ffn(x2, norm_ffn[l].reshape(1, d), w_gate_up[l].astype(BF16),
                  w_down[l].astype(BF16), norm_final.reshape(1, d), tm,
                  final_norm=(l == depth - 1))
    return xf.reshape(batch, seq, d)
```

```python
import functools

import jax
import jax.numpy as jnp
from jax import lax
from jax.experimental import pallas as pl
from jax.experimental.pallas import tpu as pltpu

F32 = jnp.float32
BF16 = jnp.bfloat16
HI = lax.Precision.HIGHEST

D_MODEL = 1024
CHUNK = 64
EPS = 1e-6
A_HEADS = 8
A_DK = 128
A_W = A_HEADS * A_DK
CONV_K = 4
B_HEADS = 16
B_DH = 64
BAND_CHUNKS = 8
BAND_LEN = (BAND_CHUNKS + 1) * CHUNK
MAX_REL = 256
N_REL = (CHUNK - 1) + MAX_REL + 1
N_REL_PAD = 384
M_HEADS = 4
M_DH = D_MODEL // M_HEADS
D_FF = 2816
AB_PAD = 128
QBLK = BAND_CHUNKS * CHUNK

COL_QKV_A, COL_ZA, COL_QB, COL_KB, COL_VB, COL_GA, COL_GB = 0, 3, 4, 5, 6, 7, 8
N_MAIN = 9 * D_MODEL

VMEM_LIMIT = 56 * 1024 * 1024


def _params(sem):
    return pltpu.CompilerParams(dimension_semantics=sem, vmem_limit_bytes=VMEM_LIMIT)


def _const_spec(shape):
    nd = len(shape)
    return pl.BlockSpec(shape, lambda *_: (0,) * nd, pipeline_mode=pl.Buffered(1))


def _dot(a, b):
    return jnp.dot(a.astype(BF16), b.astype(BF16), preferred_element_type=F32)


def _dot_nt(a, b):
    return lax.dot_general(a.astype(BF16), b.astype(BF16), (((1,), (1,)), ((), ())),
                           preferred_element_type=F32)


def _dot_tn(a, b):
    return lax.dot_general(a.astype(BF16), b.astype(BF16), (((0,), (0,)), ((), ())),
                           preferred_element_type=F32)


def _dot_hi(a, b):
    return jnp.dot(a, b, precision=HI, preferred_element_type=F32)


def _sigmoid(x):
    return 1.0 / (1.0 + jnp.exp(-x))


def _softplus(x):
    return jnp.maximum(x, 0.0) + jnp.log1p(jnp.exp(-jnp.abs(x)))


def _rmsnorm(x, g):
    return x * lax.rsqrt(jnp.mean(x * x, axis=-1, keepdims=True) + EPS) * g


CONV_COLS = 256


def _inproj_kernel(x_ref, g_ref, w_ref, wab_ref, cw_ref, o_ref, ab_ref, tail_ref, *,
                   tiles_per_seq):
    tm = x_ref.shape[0]

    @pl.when(pl.program_id(0) % tiles_per_seq == 0)
    def _():
        tail_ref[...] = jnp.zeros_like(tail_ref)

    h = _rmsnorm(x_ref[...], g_ref[...]).astype(BF16)
    ab_ref[...] = jnp.dot(h, wab_ref[...], preferred_element_type=F32)
    for lo in range(0, 3 * A_W, CONV_COLS):
        cols = slice(lo, lo + CONV_COLS)
        o = jnp.dot(h, w_ref[:, cols], preferred_element_type=F32)
        xx = jnp.concatenate([tail_ref[:, cols], o], axis=0)
        tail_ref[:, cols] = o[tm - 8:, :]
        acc = o * cw_ref[CONV_K - 1:CONV_K, cols]
        for s in range(1, CONV_K):
            shifted = pltpu.roll(xx, s, axis=0)[8:, :]
            acc = acc + shifted * cw_ref[CONV_K - 1 - s:CONV_K - s, cols]
        o_ref[:, cols] = (acc * _sigmoid(acc)).astype(o_ref.dtype)
    for lo in range(3 * A_W, N_MAIN, D_MODEL):
        cols = slice(lo, lo + D_MODEL)
        o_ref[:, cols] = jnp.dot(h, w_ref[:, cols], preferred_element_type=F32).astype(o_ref.dtype)


def _inproj(x, g, w_main, w_ab, conv_w, tm, seq):
    t = x.shape[0]
    n = w_main.shape[1]
    return pl.pallas_call(
        functools.partial(_inproj_kernel, tiles_per_seq=seq // tm),
        grid=(t // tm,),
        in_specs=[
            pl.BlockSpec((tm, D_MODEL), lambda i: (i, 0)),
            _const_spec((1, D_MODEL)),
            _const_spec((D_MODEL, n)),
            _const_spec((D_MODEL, AB_PAD)),
            _const_spec((CONV_K, 3 * A_W)),
        ],
        out_specs=[
            pl.BlockSpec((tm, n), lambda i: (i, 0)),
            pl.BlockSpec((tm, AB_PAD), lambda i: (i, 0)),
        ],
        out_shape=[
            jax.ShapeDtypeStruct((t, n), BF16),
            jax.ShapeDtypeStruct((t, AB_PAD), F32),
        ],
        scratch_shapes=[pltpu.VMEM((8, 3 * A_W), F32)],
        compiler_params=_params(("arbitrary",)),
        name="inproj",
    )(x, g, w_main, w_ab, conv_w)


def _norm_matmul_kernel(x_ref, g_ref, w_ref, o_ref):
    h = _rmsnorm(x_ref[...], g_ref[...]).astype(BF16)
    o_ref[...] = jnp.dot(h, w_ref[...], preferred_element_type=F32).astype(o_ref.dtype)


def _norm_matmul(x, g, w, tm):
    t = x.shape[0]
    n = w.shape[1]
    return pl.pallas_call(
        _norm_matmul_kernel,
        grid=(t // tm,),
        in_specs=[
            pl.BlockSpec((tm, D_MODEL), lambda i: (i, 0)),
            _const_spec((1, D_MODEL)),
            _const_spec((D_MODEL, n)),
        ],
        out_specs=pl.BlockSpec((tm, n), lambda i: (i, 0)),
        out_shape=jax.ShapeDtypeStruct((t, n), BF16),
        compiler_params=_params(("parallel",)),
        name="mem_kv",
    )(x, g, w)


DELTA_CHUNKS = 2
DELTA_ROWS = DELTA_CHUNKS * CHUNK


def _split_hi_lo(x):
    hi = x.astype(BF16)
    lo = (x - hi.astype(F32)).astype(BF16)
    return hi, lo


def _lhs_split(a, lo_half):
    hi, lo = _split_hi_lo(a)
    s = jnp.where(lo_half, hi, lo)
    return jnp.concatenate([s, s], axis=1)


def _rhs_split(b):
    hi, lo = _split_hi_lo(b)
    return jnp.concatenate([hi, hi, lo, lo], axis=0)


def _mm(a_split, b_split):
    return jnp.dot(a_split, b_split, preferred_element_type=F32)


def _delta_kernel(qkv_ref, za_ref, ab_ref, alt_ref, alog_r_ref, dt_r_ref,
                  alog_c_ref, dt_c_ref, hn_ref, o_ref, s_ref):
    rows = DELTA_ROWS

    @pl.when(pl.program_id(1) == 0)
    def _():
        s_ref[...] = jnp.zeros_like(s_ref)

    ab = ab_ref[...]
    g_c = -jnp.exp(alog_r_ref[...]) * _softplus(ab + dt_r_ref[...])
    rr = lax.broadcasted_iota(jnp.int32, (rows, rows), 0)
    cc = lax.broadcasted_iota(jnp.int32, (rows, rows), 1)
    tril_bd = jnp.logical_and(rr >= cc, (rr >> 6) == (cc >> 6)).astype(F32)
    cum_c = _dot_hi(tril_bd, g_c)
    g_r = -jnp.exp(alog_c_ref[...]) * _softplus(alt_ref[0] + dt_c_ref[...])
    rs = lax.broadcasted_iota(jnp.int32, (rows, 2 * rows), 0)
    cs = lax.broadcasted_iota(jnp.int32, (rows, 2 * rows), 1)
    triu_dup = jnp.logical_and((rs >> 6) == (cs >> 7), (rs & 63) <= (cs & 63)).astype(F32)
    cum_r = _dot_hi(g_r, triu_dup)
    beta_c = _sigmoid(ab)
    exp_cum = jnp.exp(cum_c)

    row = lax.broadcasted_iota(jnp.int32, (CHUNK, 2 * CHUNK), 0)
    lane = lax.broadcasted_iota(jnp.int32, (CHUNK, 2 * CHUNK), 1)
    col = lane & (CHUNK - 1)
    lo_half = lane < CHUNK
    incl = row >= col
    strict = row > col
    eye = (row == col).astype(F32)
    same16 = (row >> 4) == (col >> 4)
    same32 = (row >> 5) == (col >> 5)
    off16 = jnp.logical_and(same32, jnp.logical_not(same16))

    chains = [(c, h) for c in range(DELTA_CHUNKS) for h in range(A_HEADS)]

    qn, kn, vv = [], [], []
    for h in range(A_HEADS):
        q = qkv_ref[:, h * A_DK:(h + 1) * A_DK].astype(F32)
        k = qkv_ref[:, A_W + h * A_DK:A_W + (h + 1) * A_DK].astype(F32)
        qn.append(q * lax.rsqrt(jnp.sum(q * q, axis=-1, keepdims=True) + EPS) * (A_DK ** -0.5))
        kn.append(k * lax.rsqrt(jnp.sum(k * k, axis=-1, keepdims=True) + EPS))
        vv.append(qkv_ref[:, 2 * A_W + h * A_DK:2 * A_W + (h + 1) * A_DK].astype(F32))

    lmat, pmat, rhs, kdec, qdec, elast = [], [], [], [], [], []
    for c, h in chains:
        rsl = slice(c * CHUNK, (c + 1) * CHUNK)
        kc = kn[h][rsl]
        qc = qn[h][rsl]
        kq = jnp.concatenate([kc, qc], axis=0).astype(BF16)
        kcat = jnp.concatenate([kc, kc], axis=0).astype(BF16)
        prod = lax.dot_general(kq, kcat, (((1,), (1,)), ((), ())), preferred_element_type=F32)
        cum = cum_c[rsl, h:h + 1]
        diff = cum - cum_r[h:h + 1, c * 2 * CHUNK:(c + 1) * 2 * CHUNK]
        decay = jnp.where(incl, jnp.exp(jnp.where(incl, diff, 0.0)), 0.0)
        beta = beta_c[rsl, A_HEADS + h:A_HEADS + h + 1]
        eg = exp_cum[rsl, h:h + 1]
        g_last = cum_c[(c + 1) * CHUNK - 1:(c + 1) * CHUNK, h:h + 1]
        lmat.append(jnp.where(strict, beta * prod[:CHUNK] * decay, 0.0))
        pmat.append((prod[CHUNK:] * decay)[:, :CHUNK].astype(BF16))
        rhs.append(_rhs_split(jnp.concatenate([beta * vv[h][rsl], (beta * eg) * kc], axis=1)))
        qdec.append(qc * eg)
        kdec.append((kc * jnp.exp(g_last - cum)).astype(BF16))
        elast.append(jnp.exp(g_last))

    d1 = [jnp.where(same16, m, 0.0) for m in lmat]
    d1r = [_rhs_split(m) for m in d1]
    d2 = [_mm(_lhs_split(m, lo_half), r) for m, r in zip(d1, d1r)]
    d2r = [_rhs_split(m) for m in d2]
    d4 = [_mm(_lhs_split(m, lo_half), r) for m, r in zip(d2, d2r)]
    d4r = [_rhs_split(m) for m in d4]
    r0 = [eye - m for m in d1]
    r1 = [m + _mm(_lhs_split(m, lo_half), r) for m, r in zip(r0, d2r)]
    d8 = [_mm(_lhs_split(m, lo_half), r) for m, r in zip(d4, d4r)]
    r2 = [m + _mm(_lhs_split(m, lo_half), r) for m, r in zip(r1, d4r)]
    r3 = [m + _mm(_lhs_split(m, lo_half), _rhs_split(d)) for m, d in zip(r2, d8)]
    t1 = [_mm(_lhs_split(jnp.where(off16, m, 0.0), lo_half), _rhs_split(r))
          for m, r in zip(lmat, r3)]
    r4 = [r - _mm(_lhs_split(r, lo_half), _rhs_split(t)) for r, t in zip(r3, t1)]
    t2 = [_mm(_lhs_split(jnp.where(same32, 0.0, m), lo_half), _rhs_split(r))
          for m, r in zip(lmat, r4)]
    r5 = [r - _mm(_lhs_split(r, lo_half), _rhs_split(t)) for r, t in zip(r4, t2)]
    sol = [_mm(_lhs_split(r, lo_half), b) for r, b in zip(r5, rhs)]

    for c in range(DELTA_CHUNKS):
        rsl = slice(c * CHUNK, (c + 1) * CHUNK)
        for h in range(A_HEADS):
            i = c * A_HEADS + h
            lanes = slice(h * A_DK, (h + 1) * A_DK)
            s_old = s_ref[h]
            wq = jnp.concatenate([sol[i][:, A_DK:], qdec[i]], axis=0)
            wq_s = _dot(wq, s_old)
            u = (sol[i][:, :A_DK] - wq_s[:CHUNK]).astype(BF16)
            o = wq_s[CHUNK:] + jnp.dot(pmat[i], u, preferred_element_type=F32)
            s_ref[h] = s_old * elast[i] + lax.dot_general(
                kdec[i], u, (((0,), (0,)), ((), ())), preferred_element_type=F32)
            on = _rmsnorm(o, hn_ref[...])
            z = za_ref[rsl, lanes].astype(F32)
            o_ref[rsl, lanes] = (on * (z * _sigmoid(z))).astype(o_ref.dtype)


def _delta(p_main, ab, alt, alog_r, dt_r, alog_c, dt_c, head_norm, batch, nc):
    t = p_main.shape[0]
    rows = DELTA_ROWS
    return pl.pallas_call(
        _delta_kernel,
        grid=(batch, nc),
        in_specs=[
            pl.BlockSpec((rows, 3 * A_W), lambda b, n: (b * nc + n, COL_QKV_A)),
            pl.BlockSpec((rows, A_W), lambda b, n: (b * nc + n, COL_ZA)),
            pl.BlockSpec((rows, AB_PAD), lambda b, n: (b * nc + n, 0)),
            pl.BlockSpec((1, A_HEADS, rows), lambda b, n: (b * nc + n, 0, 0)),
            _const_spec((1, AB_PAD)),
            _const_spec((1, AB_PAD)),
            _const_spec((A_HEADS, 1)),
            _const_spec((A_HEADS, 1)),
            _const_spec((1, A_DK)),
        ],
        out_specs=pl.BlockSpec((rows, A_W), lambda b, n: (b * nc + n, 0)),
        out_shape=jax.ShapeDtypeStruct((t, A_W), BF16),
        scratch_shapes=[pltpu.VMEM((A_HEADS, A_DK, A_DK), F32)],
        compiler_params=_params(("parallel", "arbitrary")),
        name="delta",
    )(p_main, p_main, ab, alt, alog_r, dt_r, alog_c, dt_c, head_norm)


def _band_bias_kernel(relt_ref, o_ref):
    rows = o_ref.shape[0] // BAND_LEN
    shape = (rows, BAND_LEN, N_REL_PAD)
    i = lax.broadcasted_iota(jnp.int32, shape, 0) + pl.program_id(0) * rows
    kpos = lax.broadcasted_iota(jnp.int32, shape, 1)
    cand = lax.broadcasted_iota(jnp.int32, shape, 2)
    r = (BAND_CHUNKS - (kpos >> 6)) * CHUNK + i - (kpos & (CHUNK - 1))
    idx = jnp.clip(r, -(CHUNK - 1), MAX_REL) + (CHUNK - 1)
    onehot = (idx == cand).astype(BF16).reshape(rows * BAND_LEN, N_REL_PAD)
    rel = relt_ref[...]
    r1 = rel.astype(BF16)
    rem = rel - r1.astype(F32)
    r2 = rem.astype(BF16)
    r3 = (rem - r2.astype(F32)).astype(BF16)
    acc = jnp.dot(onehot, r1, preferred_element_type=F32)
    acc = acc + jnp.dot(onehot, r2, preferred_element_type=F32)
    acc = acc + jnp.dot(onehot, r3, preferred_element_type=F32)
    o_ref[...] = acc


def _band_bias(rel_bias):
    relt = jnp.pad(rel_bias.T, ((0, N_REL_PAD - N_REL), (0, 0)))
    rows = 8
    out = pl.pallas_call(
        _band_bias_kernel,
        grid=(CHUNK // rows,),
        in_specs=[_const_spec((N_REL_PAD, B_HEADS))],
        out_specs=pl.BlockSpec((rows * BAND_LEN, B_HEADS), lambda g: (g, 0)),
        out_shape=jax.ShapeDtypeStruct((CHUNK * BAND_LEN, B_HEADS), F32),
        compiler_params=_params(("parallel",)),
        name="band_bias",
    )(relt)
    return out.reshape(CHUNK, BAND_LEN, B_HEADS).transpose(2, 0, 1)


def _band_kernel(q_ref, kp_ref, kc_ref, vp_ref, vc_ref, bias_ref, o_ref, kwin, vwin):
    blk = pl.program_id(1)
    kwin[:QBLK, :] = kp_ref[...]
    kwin[QBLK:, :] = kc_ref[...]
    vwin[:QBLK, :] = vp_ref[...]
    vwin[QBLK:, :] = vc_ref[...]
    col_chunk = lax.broadcasted_iota(jnp.int32, (2 * CHUNK, BAND_LEN), 1) >> 6
    lane = lax.broadcasted_iota(jnp.int32, (CHUNK, 2 * B_DH), 1)
    even = lane < B_DH
    scale = B_DH ** -0.5
    pairs = B_HEADS // 2
    group = 4

    def body(a, carry):
        r0 = pl.multiple_of(a * CHUNK, CHUNK)
        valid = (col_chunk + a + BAND_CHUNKS * blk) >= BAND_CHUNKS
        for g0 in range(0, pairs, group):
            sc = []
            for pr in range(g0, g0 + group):
                lanes = slice(pr * 2 * B_DH, (pr + 1) * 2 * B_DH)
                qp = q_ref[pl.ds(r0, CHUNK), lanes] * scale
                q2 = jnp.concatenate([jnp.where(even, qp, 0), jnp.where(even, 0, qp)], axis=0)
                kp = kwin[pl.ds(r0, BAND_LEN), lanes]
                bias2 = jnp.concatenate([bias_ref[2 * pr], bias_ref[2 * pr + 1]], axis=0)
                s = lax.dot_general(q2.astype(BF16), kp, (((1,), (1,)), ((), ())),
                                    preferred_element_type=F32) + bias2
                sc.append(jnp.where(valid, s, -1e30))
            m = [jnp.max(s, axis=-1, keepdims=True) for s in sc]
            p = [jnp.exp(s - mm) for s, mm in zip(sc, m)]
            denom = [jnp.sum(pp, axis=-1, keepdims=True) for pp in p]
            for i, pr in enumerate(range(g0, g0 + group)):
                lanes = slice(pr * 2 * B_DH, (pr + 1) * 2 * B_DH)
                vp = vwin[pl.ds(r0, BAND_LEN), lanes]
                o2 = jnp.dot(p[i].astype(BF16), vp, preferred_element_type=F32) / denom[i]
                o_ref[pl.ds(r0, CHUNK), lanes] = jnp.where(
                    even, o2[:CHUNK], o2[CHUNK:]).astype(o_ref.dtype)
        return carry

    lax.fori_loop(0, QBLK // CHUNK, body, 0)


def _band(p_main, bias, batch, nblk):
    t = p_main.shape[0]

    def prev(b, i):
        return b * nblk + jnp.maximum(i - 1, 0)

    return pl.pallas_call(
        _band_kernel,
        grid=(batch, nblk),
        in_specs=[
            pl.BlockSpec((QBLK, D_MODEL), lambda b, i: (b * nblk + i, COL_QB)),
            pl.BlockSpec((QBLK, D_MODEL), lambda b, i: (prev(b, i), COL_KB)),
            pl.BlockSpec((QBLK, D_MODEL), lambda b, i: (b * nblk + i, COL_KB)),
            pl.BlockSpec((QBLK, D_MODEL), lambda b, i: (prev(b, i), COL_VB)),
            pl.BlockSpec((QBLK, D_MODEL), lambda b, i: (b * nblk + i, COL_VB)),
            _const_spec((B_HEADS, CHUNK, BAND_LEN)),
        ],
        out_specs=pl.BlockSpec((QBLK, D_MODEL), lambda b, i: (b * nblk + i, 0)),
        out_shape=jax.ShapeDtypeStruct((t, D_MODEL), BF16),
        scratch_shapes=[
            pltpu.VMEM((2 * QBLK, D_MODEL), BF16),
            pltpu.VMEM((2 * QBLK, D_MODEL), BF16),
        ],
        compiler_params=_params(("parallel", "parallel")),
        name="band",
    )(p_main, p_main, p_main, p_main, p_main, bias)


def _merge_kernel(oa_ref, ob_ref, ga_ref, gb_ref, x_ref, wa_ref, wb_ref, wo_ref, o_ref):
    ya = jnp.dot(oa_ref[...], wa_ref[...], preferred_element_type=F32)
    yb = jnp.dot(ob_ref[...], wb_ref[...], preferred_element_type=F32)
    y = _sigmoid(ga_ref[...].astype(F32)) * ya + _sigmoid(gb_ref[...].astype(F32)) * yb
    o_ref[...] = x_ref[...] + jnp.dot(y.astype(BF16), wo_ref[...], preferred_element_type=F32)


def _merge(oa, ob, p_main, x, wa, wb, wo, tm):
    t = x.shape[0]
    row = lambda i: (i, 0)
    return pl.pallas_call(
        _merge_kernel,
        grid=(t // tm,),
        in_specs=[
            pl.BlockSpec((tm, D_MODEL), row),
            pl.BlockSpec((tm, D_MODEL), row),
            pl.BlockSpec((tm, D_MODEL), lambda i: (i, COL_GA)),
            pl.BlockSpec((tm, D_MODEL), lambda i: (i, COL_GB)),
            pl.BlockSpec((tm, D_MODEL), row),
            _const_spec((D_MODEL, D_MODEL)),
            _const_spec((D_MODEL, D_MODEL)),
            _const_spec((D_MODEL, D_MODEL)),
        ],
        out_specs=pl.BlockSpec((tm, D_MODEL), row),
        out_shape=jax.ShapeDtypeStruct((t, D_MODEL), F32),
        compiler_params=_params(("parallel",)),
        name="merge",
    )(oa, ob, p_main, p_main, x, wa, wb, wo)


def _xattn_kernel(x_ref, g_ref, wq_ref, kv_ref, wo_ref, o_ref):
    x = x_ref[...]
    h = _rmsnorm(x, g_ref[...]).astype(BF16)
    q = jnp.dot(h, wq_ref[...], preferred_element_type=F32).astype(BF16)
    scale = M_DH ** -0.5
    outs = []
    for hh in range(M_HEADS):
        kh = kv_ref[:, hh * M_DH:(hh + 1) * M_DH]
        vh = kv_ref[:, D_MODEL + hh * M_DH:D_MODEL + (hh + 1) * M_DH]
        sc = _dot_nt(q[:, hh * M_DH:(hh + 1) * M_DH], kh) * scale
        m = jnp.max(sc, axis=-1, keepdims=True)
        p = jnp.exp(sc - m)
        denom = jnp.sum(p, axis=-1, keepdims=True)
        outs.append((_dot(p, vh) / denom).astype(BF16))
    o = jnp.concatenate(outs, axis=1)
    o_ref[...] = x + jnp.dot(o, wo_ref[...], preferred_element_type=F32)


def _xattn(x, g, wq, kv, wo, tm, seq, mem_len):
    t = x.shape[0]
    per_batch = seq // tm
    return pl.pallas_call(
        _xattn_kernel,
        grid=(t // tm,),
        in_specs=[
            pl.BlockSpec((tm, D_MODEL), lambda i: (i, 0)),
            _const_spec((1, D_MODEL)),
            _const_spec((D_MODEL, D_MODEL)),
            pl.BlockSpec((mem_len, 2 * D_MODEL), lambda i: (i // per_batch, 0)),
            _const_spec((D_MODEL, D_MODEL)),
        ],
        out_specs=pl.BlockSpec((tm, D_MODEL), lambda i: (i, 0)),
        out_shape=jax.ShapeDtypeStruct((t, D_MODEL), F32),
        compiler_params=_params(("parallel",)),
        name="xattn",
    )(x, g, wq, kv, wo)


FF_CHUNK = 256


def _ffn_kernel(x_ref, g_ref, wgu_ref, wd_ref, gf_ref, o_ref, act_ref, *, final_norm):
    x = x_ref[...]
    h = _rmsnorm(x, g_ref[...]).astype(BF16)
    for c in range(D_FF // FF_CHUNK):
        lo = c * FF_CHUNK
        gate = jnp.dot(h, wgu_ref[:, lo:lo + FF_CHUNK], preferred_element_type=F32)
        up = jnp.dot(h, wgu_ref[:, D_FF + lo:D_FF + lo + FF_CHUNK], preferred_element_type=F32)
        act_ref[:, lo:lo + FF_CHUNK] = (gate * _sigmoid(gate) * up).astype(BF16)
    y = x + jnp.dot(act_ref[...], wd_ref[...], preferred_element_type=F32)
    if final_norm:
        y = _rmsnorm(y, gf_ref[...])
    o_ref[...] = y


def _ffn(x, g, wgu, wd, gf, tm, final_norm):
    t = x.shape[0]
    return pl.pallas_call(
        functools.partial(_ffn_kernel, final_norm=final_norm),
        grid=(t // tm,),
        in_specs=[
            pl.BlockSpec((tm, D_MODEL), lambda i: (i, 0)),
            _const_spec((1, D_MODEL)),
            _const_spec((D_MODEL, 2 * D_FF)),
            _const_spec((D_FF, D_MODEL)),
            _const_spec((1, D_MODEL)),
        ],
        out_specs=pl.BlockSpec((tm, D_MODEL), lambda i: (i, 0)),
        out_shape=jax.ShapeDtypeStruct((t, D_MODEL), F32),
        scratch_shapes=[pltpu.VMEM((tm, D_FF), BF16)],
        compiler_params=_params(("parallel",)),
        name="ffn",
    )(x, g, wgu, wd, gf)


def kernel(x, mem, norm_mix, w_in, conv_w, a_log, dt_bias, head_norm, w_a_out, w_b_out,
           rel_bias, w_o, norm_xattn, norm_mem, w_mq, w_mkv, w_mo, norm_ffn, w_gate_up,
           w_down, norm_final):
    batch, seq, d = x.shape
    mem_len = mem.shape[1]
    depth = w_in.shape[0]
    t = batch * seq
    nsteps = seq // DELTA_ROWS
    nblk = seq // QBLK
    tm = min(512, seq)

    xf = x.reshape(t, d)
    memf = mem.reshape(batch * mem_len, d)
    bias = _band_bias(rel_bias)
    ab_lo = 4 * A_W
    ab_hi = ab_lo + 2 * A_HEADS

    def pad_lanes(v):
        return jnp.pad(v.reshape(1, -1), ((0, 0), (0, AB_PAD - v.shape[-1])))

    for l in range(depth):
        w_l = w_in[l]
        w_main = jnp.concatenate([w_l[:, :ab_lo], w_l[:, ab_hi:]], axis=1).astype(BF16)
        w_ab = jnp.pad(w_l[:, ab_lo:ab_hi], ((0, 0), (0, AB_PAD - 2 * A_HEADS))).astype(BF16)
        p_main, ab = _inproj(xf, norm_mix[l].reshape(1, d), w_main, w_ab, conv_w[l], tm, seq)

        alt = ab[:, :A_HEADS].reshape(batch * nsteps, DELTA_ROWS, A_HEADS).transpose(0, 2, 1)
        oa = _delta(p_main, ab, alt, pad_lanes(a_log[l]), pad_lanes(dt_bias[l]),
                    a_log[l].reshape(A_HEADS, 1), dt_bias[l].reshape(A_HEADS, 1),
                    head_norm[l].reshape(1, A_DK), batch, nsteps)
        ob = _band(p_main, bias, batch, nblk)
        x1 = _merge(oa, ob, p_main, xf, w_a_out[l].astype(BF16), w_b_out[l].astype(BF16),
                    w_o[l].astype(BF16), tm)

        kv = _norm_matmul(memf, norm_mem[l].reshape(1, d), w_mkv[l].astype(BF16),
                          min(512, batch * mem_len))
        x2 = _xattn(x1, norm_xattn[l].reshape(1, d), w_mq[l].astype(BF16), kv,
                    w_mo[l].astype(BF16), tm, seq, mem_len)

        xf = _ffn(x2, norm_ffn[l].reshape(1, d), w_gate_up[l].astype(BF16),
                  w_down[l].astype(BF16), norm_final.reshape(1, d), tm,
                  final_norm=(l == depth - 1))
    return xf.reshape(batch, seq, d)
```

```python
import functools

import jax
import jax.numpy as jnp
from jax import lax
from jax.experimental import pallas as pl
from jax.experimental.pallas import tpu as pltpu

F32 = jnp.float32
BF16 = jnp.bfloat16
HI = lax.Precision.HIGHEST

D_MODEL = 1024
CHUNK = 64
EPS = 1e-6
A_HEADS = 8
A_DK = 128
A_W = A_HEADS * A_DK
CONV_K = 4
B_HEADS = 16
B_DH = 64
BAND_CHUNKS = 8
BAND_LEN = (BAND_CHUNKS + 1) * CHUNK
MAX_REL = 256
N_REL = (CHUNK - 1) + MAX_REL + 1
N_REL_PAD = 384
M_HEADS = 4
M_DH = D_MODEL // M_HEADS
D_FF = 2816
AB_PAD = 128
QBLK = BAND_CHUNKS * CHUNK
BAND_WIN = BAND_LEN + CHUNK
BAND_GROUP = 2
LOG2E = 1.4426950408889634
QB_SCALE = (B_DH ** -0.5) * LOG2E

COL_QKV_A, COL_ZA, COL_QB, COL_KB, COL_VB, COL_GA, COL_GB = 0, 3, 4, 5, 6, 7, 8
N_MAIN = 9 * D_MODEL

VMEM_LIMIT = 56 * 1024 * 1024


def _params(sem):
    return pltpu.CompilerParams(dimension_semantics=sem, vmem_limit_bytes=VMEM_LIMIT)


def _const_spec(shape):
    nd = len(shape)
    return pl.BlockSpec(shape, lambda *_: (0,) * nd, pipeline_mode=pl.Buffered(1))


def _dot(a, b):
    return jnp.dot(a.astype(BF16), b.astype(BF16), preferred_element_type=F32)


def _dot_nt(a, b):
    return lax.dot_general(a.astype(BF16), b.astype(BF16), (((1,), (1,)), ((), ())),
                           preferred_element_type=F32)


def _dot_tn(a, b):
    return lax.dot_general(a.astype(BF16), b.astype(BF16), (((0,), (0,)), ((), ())),
                           preferred_element_type=F32)


def _dot_hi(a, b):
    return jnp.dot(a, b, precision=HI, preferred_element_type=F32)


def _sigmoid(x):
    return 1.0 / (1.0 + jnp.exp(-x))


def _softplus(x):
    return jnp.maximum(x, 0.0) + jnp.log1p(jnp.exp(-jnp.abs(x)))


def _rmsnorm(x, g):
    return x * lax.rsqrt(jnp.mean(x * x, axis=-1, keepdims=True) + EPS) * g


CONV_COLS = 256


def _inproj_kernel(x_ref, g_ref, w_ref, wab_ref, cw_ref, o_ref, ab_ref, tail_ref, *,
                   tiles_per_seq):
    tm = x_ref.shape[0]

    @pl.when(pl.program_id(0) % tiles_per_seq == 0)
    def _():
        tail_ref[...] = jnp.zeros_like(tail_ref)

    h = _rmsnorm(x_ref[...], g_ref[...]).astype(BF16)
    ab_ref[...] = jnp.dot(h, wab_ref[...], preferred_element_type=F32)
    for lo in range(0, 3 * A_W, CONV_COLS):
        cols = slice(lo, lo + CONV_COLS)
        o = jnp.dot(h, w_ref[:, cols], preferred_element_type=F32)
        xx = jnp.concatenate([tail_ref[:, cols], o], axis=0)
        tail_ref[:, cols] = o[tm - 8:, :]
        acc = o * cw_ref[CONV_K - 1:CONV_K, cols]
        for s in range(1, CONV_K):
            shifted = pltpu.roll(xx, s, axis=0)[8:, :]
            acc = acc + shifted * cw_ref[CONV_K - 1 - s:CONV_K - s, cols]
        o_ref[:, cols] = (acc * _sigmoid(acc)).astype(o_ref.dtype)
    for lo in range(3 * A_W, N_MAIN, D_MODEL):
        cols = slice(lo, lo + D_MODEL)
        o = jnp.dot(h, w_ref[:, cols], preferred_element_type=F32)
        if lo == COL_QB * D_MODEL:
            o = o * QB_SCALE
        o_ref[:, cols] = o.astype(o_ref.dtype)


def _inproj(x, g, w_main, w_ab, conv_w, tm, seq):
    t = x.shape[0]
    n = w_main.shape[1]
    return pl.pallas_call(
        functools.partial(_inproj_kernel, tiles_per_seq=seq // tm),
        grid=(t // tm,),
        in_specs=[
            pl.BlockSpec((tm, D_MODEL), lambda i: (i, 0)),
            _const_spec((1, D_MODEL)),
            _const_spec((D_MODEL, n)),
            _const_spec((D_MODEL, AB_PAD)),
            _const_spec((CONV_K, 3 * A_W)),
        ],
        out_specs=[
            pl.BlockSpec((tm, n), lambda i: (i, 0)),
            pl.BlockSpec((tm, AB_PAD), lambda i: (i, 0)),
        ],
        out_shape=[
            jax.ShapeDtypeStruct((t, n), BF16),
            jax.ShapeDtypeStruct((t, AB_PAD), F32),
        ],
        scratch_shapes=[pltpu.VMEM((8, 3 * A_W), F32)],
        compiler_params=_params(("arbitrary",)),
        name="inproj",
    )(x, g, w_main, w_ab, conv_w)


def _norm_matmul_kernel(x_ref, g_ref, w_ref, o_ref):
    h = _rmsnorm(x_ref[...], g_ref[...]).astype(BF16)
    o_ref[...] = jnp.dot(h, w_ref[...], preferred_element_type=F32).astype(o_ref.dtype)


def _norm_matmul(x, g, w, tm):
    t = x.shape[0]
    n = w.shape[1]
    return pl.pallas_call(
        _norm_matmul_kernel,
        grid=(t // tm,),
        in_specs=[
            pl.BlockSpec((tm, D_MODEL), lambda i: (i, 0)),
            _const_spec((1, D_MODEL)),
            _const_spec((D_MODEL, n)),
        ],
        out_specs=pl.BlockSpec((tm, n), lambda i: (i, 0)),
        out_shape=jax.ShapeDtypeStruct((t, n), BF16),
        compiler_params=_params(("parallel",)),
        name="mem_kv",
    )(x, g, w)


DELTA_CHUNKS = 4
DELTA_ROWS = DELTA_CHUNKS * CHUNK


def _split_hi_lo(x):
    hi = x.astype(BF16)
    lo = (x - hi.astype(F32)).astype(BF16)
    return hi, lo


def _pair_lhs(split):
    hi, lo = split
    return jnp.concatenate([hi, hi, lo], axis=1)


def _pair_rhs(split, lo_half):
    def bd(x):
        return jnp.concatenate([jnp.where(lo_half, x, 0), jnp.where(lo_half, 0, x)], axis=0)

    hi, lo = split
    return jnp.concatenate([bd(hi), bd(lo), bd(hi)], axis=0)


def _pair_mask(split, mask):
    hi, lo = split
    return jnp.where(mask, hi, 0), jnp.where(mask, lo, 0)


def _lhs_split(a, lo_half):
    hi, lo = _split_hi_lo(a)
    s = jnp.where(lo_half, hi, lo)
    return jnp.concatenate([s, s], axis=1)


def _rhs_split(b):
    hi, lo = _split_hi_lo(b)
    return jnp.concatenate([hi, hi, lo, lo], axis=0)


def _mm(a_split, b_split):
    return jnp.dot(a_split, b_split, preferred_element_type=F32)


def _delta_kernel(qkv_ref, za_ref, ab_ref, alt_ref, alog_r_ref, dt_r_ref,
                  alog_c_ref, dt_c_ref, hn_ref, o_ref, s_ref):
    rows = DELTA_ROWS

    @pl.when(pl.program_id(1) == 0)
    def _():
        s_ref[...] = jnp.zeros_like(s_ref)

    ab = ab_ref[...]
    g_c = -jnp.exp(alog_r_ref[...]) * _softplus(ab + dt_r_ref[...])
    rr = lax.broadcasted_iota(jnp.int32, (rows, rows), 0)
    cc = lax.broadcasted_iota(jnp.int32, (rows, rows), 1)
    tril_bd = jnp.logical_and(rr >= cc, (rr >> 6) == (cc >> 6)).astype(F32)
    cum_c = _dot_hi(tril_bd, g_c)
    g_r = -jnp.exp(alog_c_ref[...]) * _softplus(alt_ref[0] + dt_c_ref[...])
    rs = lax.broadcasted_iota(jnp.int32, (rows, 2 * rows), 0)
    cs = lax.broadcasted_iota(jnp.int32, (rows, 2 * rows), 1)
    upto = jnp.logical_and((rs >> 6) == (cs >> 7), (rs & 63) <= (cs & 63))
    first = (cs & CHUNK) == 0
    tri_even = jnp.logical_and(upto, first).astype(F32)
    tri_odd = jnp.logical_and(upto, jnp.logical_not(first)).astype(F32)
    npair = A_HEADS // 2
    cum_rp = _dot_hi(g_r[:npair], tri_even) + _dot_hi(g_r[npair:], tri_odd)
    beta_c = _sigmoid(ab)
    exp_cum = jnp.exp(cum_c)

    row = lax.broadcasted_iota(jnp.int32, (CHUNK, 2 * CHUNK), 0)
    lane = lax.broadcasted_iota(jnp.int32, (CHUNK, 2 * CHUNK), 1)
    col = lane & (CHUNK - 1)
    lo_half = lane < CHUNK
    incl = row >= col
    strict = row > col
    eye = (row == col).astype(F32)
    same16 = (row >> 4) == (col >> 4)
    same32 = (row >> 5) == (col >> 5)
    off16 = jnp.logical_and(same32, jnp.logical_not(same16))

    eye_bf = eye.astype(BF16)
    zero_blk = jnp.zeros((CHUNK, A_DK), F32)

    qn, kn, vv = [], [], []
    for h in range(A_HEADS):
        q = qkv_ref[:, h * A_DK:(h + 1) * A_DK].astype(F32)
        k = qkv_ref[:, A_W + h * A_DK:A_W + (h + 1) * A_DK].astype(F32)
        qn.append(q * lax.rsqrt(jnp.sum(q * q, axis=-1, keepdims=True) + EPS) * (A_DK ** -0.5))
        kn.append(k * lax.rsqrt(jnp.sum(k * k, axis=-1, keepdims=True) + EPS))
        vv.append(qkv_ref[:, 2 * A_W + h * A_DK:2 * A_W + (h + 1) * A_DK].astype(F32))

    lmat, pmat = [], []
    for c in range(DELTA_CHUNKS):
        rsl = slice(c * CHUNK, (c + 1) * CHUNK)
        for i in range(npair):
            a, b = 2 * i, 2 * i + 1
            ka, kb = kn[a][rsl], kn[b][rsl]
            lhs = jnp.concatenate([jnp.concatenate([ka, kb], axis=1),
                                   jnp.concatenate([qn[a][rsl], qn[b][rsl]], axis=1)], axis=0)
            rhs_nt = jnp.concatenate([jnp.concatenate([ka, zero_blk], axis=1),
                                      jnp.concatenate([zero_blk, kb], axis=1)], axis=0)
            prod = lax.dot_general(lhs.astype(BF16), rhs_nt.astype(BF16),
                                   (((1,), (1,)), ((), ())), preferred_element_type=F32)
            cum2 = jnp.where(lo_half, cum_c[rsl, a:a + 1], cum_c[rsl, b:b + 1])
            diff = cum2 - cum_rp[i:i + 1, c * 2 * CHUNK:(c + 1) * 2 * CHUNK]
            decay = jnp.where(incl, jnp.exp(jnp.where(incl, diff, 0.0)), 0.0)
            beta2 = jnp.where(lo_half, beta_c[rsl, A_HEADS + a:A_HEADS + a + 1],
                              beta_c[rsl, A_HEADS + b:A_HEADS + b + 1])
            lmat.append(jnp.where(strict, beta2 * prod[:CHUNK] * decay, 0.0))
            pm = prod[CHUNK:] * decay
            pmat.append((jnp.where(lo_half, pm, 0.0).astype(BF16),
                         jnp.where(lo_half, 0.0, pm).astype(BF16)))

    rhs, kdec, qdec, elast = [], [], [], []
    for c in range(DELTA_CHUNKS):
        rsl = slice(c * CHUNK, (c + 1) * CHUNK)
        for h in range(A_HEADS):
            kc = kn[h][rsl]
            cum = cum_c[rsl, h:h + 1]
            beta = beta_c[rsl, A_HEADS + h:A_HEADS + h + 1]
            eg = exp_cum[rsl, h:h + 1]
            g_last = cum_c[(c + 1) * CHUNK - 1:(c + 1) * CHUNK, h:h + 1]
            rhs.append(_rhs_split(jnp.concatenate([beta * vv[h][rsl], (beta * eg) * kc], axis=1)))
            qdec.append(qn[h][rsl] * eg)
            kdec.append((kc * jnp.exp(g_last - cum)).astype(BF16))
            elast.append(jnp.exp(g_last))

    def mul(xs, ys):
        return [_mm(_pair_lhs(x), _pair_rhs(y, lo_half)) for x, y in zip(xs, ys)]

    def split(ms):
        return [_split_hi_lo(m) for m in ms]

    l_s = split(lmat)
    d1 = [_pair_mask(s, same16) for s in l_s]
    d2f = mul(d1, d1)
    d2 = split(d2f)
    d4 = split(mul(d2, d2))
    r0 = [(eye_bf - hi, -lo) for hi, lo in d1]
    r0f = [eye - jnp.where(same16, m, 0.0) for m in lmat]
    r1f = [r + m for r, m in zip(r0f, mul(r0, d2))]
    d8 = split(mul(d4, d4))
    r1 = split(r1f)
    r2f = [r + m for r, m in zip(r1f, mul(r1, d4))]
    r2 = split(r2f)
    r3f = [r + m for r, m in zip(r2f, mul(r2, d8))]
    r3 = split(r3f)
    t1 = split(mul([_pair_mask(s, off16) for s in l_s], r3))
    r4f = [r - m for r, m in zip(r3f, mul(r3, t1))]
    r4 = split(r4f)
    t2 = split(mul([_pair_mask(s, jnp.logical_not(same32)) for s in l_s], r4))
    r5f = [r - m for r, m in zip(r4f, mul(r4, t2))]

    sol = []
    for idx, t_pair in enumerate(r5f):
        c, i = divmod(idx, npair)
        swapped = pltpu.roll(t_pair, CHUNK, axis=1)
        for h, t_dup in ((2 * i, jnp.where(lo_half, t_pair, swapped)),
                         (2 * i + 1, jnp.where(lo_half, swapped, t_pair))):
            sol.append((c * A_HEADS + h, _mm(_lhs_split(t_dup, lo_half), rhs[c * A_HEADS + h])))
    sol = [s for _, s in sorted(sol, key=lambda e: e[0])]

    lin, s_add, o_add = [], [], []
    for idx, wu in enumerate(sol):
        c, h = divmod(idx, A_HEADS)
        wu_bf = wu.astype(BF16)
        g = lax.dot_general(kdec[idx], wu_bf, (((0,), (0,)), ((), ())),
                            preferred_element_type=F32)
        pm = pmat[c * npair + h // 2][h % 2]
        pw = jnp.dot(pm, jnp.concatenate([wu_bf, wu_bf], axis=0),
                     preferred_element_type=F32)
        lin.append(jnp.concatenate([g[:, A_DK:], qdec[idx] - pw[:, A_DK:]], axis=0).astype(BF16))
        s_add.append(g[:, :A_DK])
        o_add.append(pw[:, :A_DK])

    state = [s_ref[h] for h in range(A_HEADS)]
    for c in range(DELTA_CHUNKS):
        rsl = slice(c * CHUNK, (c + 1) * CHUNK)
        for h in range(A_HEADS):
            i = c * A_HEADS + h
            lanes = slice(h * A_DK, (h + 1) * A_DK)
            y = jnp.dot(lin[i], state[h].astype(BF16), preferred_element_type=F32)
            state[h] = state[h] * elast[i] - y[:A_DK] + s_add[i]
            o = y[A_DK:] + o_add[i]
            on = _rmsnorm(o, hn_ref[...])
            z = za_ref[rsl, lanes].astype(F32)
            o_ref[rsl, lanes] = (on * (z * _sigmoid(z))).astype(o_ref.dtype)
    for h in range(A_HEADS):
        s_ref[h] = state[h]


def _delta(p_main, ab, alt, alog_r, dt_r, alog_c, dt_c, head_norm, batch, nc):
    t = p_main.shape[0]
    rows = DELTA_ROWS
    return pl.pallas_call(
        _delta_kernel,
        grid=(batch, nc),
        in_specs=[
            pl.BlockSpec((rows, 3 * A_W), lambda b, n: (b * nc + n, COL_QKV_A)),
            pl.BlockSpec((rows, A_W), lambda b, n: (b * nc + n, COL_ZA)),
            pl.BlockSpec((rows, AB_PAD), lambda b, n: (b * nc + n, 0)),
            pl.BlockSpec((1, A_HEADS, rows), lambda b, n: (b * nc + n, 0, 0)),
            _const_spec((1, AB_PAD)),
            _const_spec((1, AB_PAD)),
            _const_spec((A_HEADS, 1)),
            _const_spec((A_HEADS, 1)),
            _const_spec((1, A_DK)),
        ],
        out_specs=pl.BlockSpec((rows, A_W), lambda b, n: (b * nc + n, 0)),
        out_shape=jax.ShapeDtypeStruct((t, A_W), BF16),
        scratch_shapes=[pltpu.VMEM((A_HEADS, A_DK, A_DK), F32)],
        compiler_params=_params(("parallel", "arbitrary")),
        name="delta",
    )(p_main, p_main, ab, alt, alog_r, dt_r, alog_c, dt_c, head_norm)


def _band_bias_kernel(relt_ref, o_ref):
    rows = o_ref.shape[0] // BAND_LEN
    shape = (rows, BAND_LEN, N_REL_PAD)
    i = lax.broadcasted_iota(jnp.int32, shape, 0) + pl.program_id(0) * rows
    kpos = lax.broadcasted_iota(jnp.int32, shape, 1)
    cand = lax.broadcasted_iota(jnp.int32, shape, 2)
    r = (BAND_CHUNKS - (kpos >> 6)) * CHUNK + i - (kpos & (CHUNK - 1))
    idx = jnp.clip(r, -(CHUNK - 1), MAX_REL) + (CHUNK - 1)
    onehot = (idx == cand).astype(BF16).reshape(rows * BAND_LEN, N_REL_PAD)
    rel = relt_ref[...]
    r1 = rel.astype(BF16)
    rem = rel - r1.astype(F32)
    r2 = rem.astype(BF16)
    r3 = (rem - r2.astype(F32)).astype(BF16)
    acc = jnp.dot(onehot, r1, preferred_element_type=F32)
    acc = acc + jnp.dot(onehot, r2, preferred_element_type=F32)
    acc = acc + jnp.dot(onehot, r3, preferred_element_type=F32)
    o_ref[...] = acc * LOG2E


def _band_bias(rel_bias):
    relt = jnp.pad(rel_bias.T, ((0, N_REL_PAD - N_REL), (0, 0)))
    rows = 8
    out = pl.pallas_call(
        _band_bias_kernel,
        grid=(CHUNK // rows,),
        in_specs=[_const_spec((N_REL_PAD, B_HEADS))],
        out_specs=pl.BlockSpec((rows * BAND_LEN, B_HEADS), lambda g: (g, 0)),
        out_shape=jax.ShapeDtypeStruct((CHUNK * BAND_LEN, B_HEADS), F32),
        compiler_params=_params(("parallel",)),
        name="band_bias",
    )(relt)
    return out.reshape(CHUNK, BAND_LEN, B_HEADS).transpose(2, 0, 1)


def _band_kernel(q_ref, kp_ref, kc_ref, vp_ref, vc_ref, bias_ref, o_ref, kwin, vwin):
    blk = pl.program_id(1)
    kwin[:QBLK, :] = kp_ref[...]
    kwin[QBLK:, :] = kc_ref[...]
    vwin[:QBLK, :] = vp_ref[...]
    vwin[QBLK:, :] = vc_ref[...]
    col_chunk = lax.broadcasted_iota(jnp.int32, (4 * CHUNK, BAND_WIN), 1) >> 6
    lane = lax.broadcasted_iota(jnp.int32, (CHUNK, 2 * B_DH), 1)
    even = lane < B_DH
    pairs = B_HEADS // 2

    def body(a2, carry, masked):
        r0 = pl.multiple_of(a2 * 2 * CHUNK, 2 * CHUNK)
        for g0 in range(0, pairs, BAND_GROUP):
            sc = []
            for pr in range(g0, g0 + BAND_GROUP):
                lanes = slice(pr * 2 * B_DH, (pr + 1) * 2 * B_DH)
                qa = q_ref[pl.ds(r0, 2 * CHUNK), lanes]
                q4 = jnp.concatenate(
                    [jnp.where(even, qa[:CHUNK], 0), jnp.where(even, 0, qa[:CHUNK]),
                     jnp.where(even, qa[CHUNK:], 0), jnp.where(even, 0, qa[CHUNK:])], axis=0)
                kp = kwin[pl.ds(r0, BAND_WIN), lanes]
                s = lax.dot_general(q4, kp, (((1,), (1,)), ((), ())),
                                    preferred_element_type=F32) + bias_ref[pr]
                if masked:
                    s = jnp.where(col_chunk + 2 * a2 >= BAND_CHUNKS, s, -1e30)
                sc.append(s)
            m = [jnp.max(s, axis=-1, keepdims=True) for s in sc]
            p = [jnp.exp2(s - mm) for s, mm in zip(sc, m)]
            denom = [jnp.sum(pp, axis=-1, keepdims=True) for pp in p]
            for i, pr in enumerate(range(g0, g0 + BAND_GROUP)):
                lanes = slice(pr * 2 * B_DH, (pr + 1) * 2 * B_DH)
                vp = vwin[pl.ds(r0, BAND_WIN), lanes]
                o4 = jnp.dot(p[i].astype(BF16), vp, preferred_element_type=F32) / denom[i]
                o2 = jnp.concatenate(
                    [jnp.where(even, o4[:CHUNK], o4[CHUNK:2 * CHUNK]),
                     jnp.where(even, o4[2 * CHUNK:3 * CHUNK], o4[3 * CHUNK:])], axis=0)
                o_ref[pl.ds(r0, 2 * CHUNK), lanes] = o2.astype(o_ref.dtype)
        return carry

    steps = QBLK // (2 * CHUNK)

    @pl.when(blk == 0)
    def _():
        lax.fori_loop(0, steps, functools.partial(body, masked=True), 0)

    @pl.when(blk > 0)
    def _():
        lax.fori_loop(0, steps, functools.partial(body, masked=False), 0)


def _band_tables(bias):
    neg = jnp.full((B_HEADS, CHUNK, CHUNK), -1e30, F32)
    first = jnp.concatenate([bias, neg], axis=2)
    second = jnp.concatenate([neg, bias], axis=2)
    t = jnp.stack([first[0::2], first[1::2], second[0::2], second[1::2]], axis=1)
    return t.reshape(B_HEADS // 2, 4 * CHUNK, BAND_WIN)


def _band(p_main, bias, batch, nblk):
    t = p_main.shape[0]

    def prev(b, i):
        return b * nblk + jnp.maximum(i - 1, 0)

    return pl.pallas_call(
        _band_kernel,
        grid=(batch, nblk),
        in_specs=[
            pl.BlockSpec((QBLK, D_MODEL), lambda b, i: (b * nblk + i, COL_QB)),
            pl.BlockSpec((QBLK, D_MODEL), lambda b, i: (prev(b, i), COL_KB)),
            pl.BlockSpec((QBLK, D_MODEL), lambda b, i: (b * nblk + i, COL_KB)),
            pl.BlockSpec((QBLK, D_MODEL), lambda b, i: (prev(b, i), COL_VB)),
            pl.BlockSpec((QBLK, D_MODEL), lambda b, i: (b * nblk + i, COL_VB)),
            _const_spec((B_HEADS // 2, 4 * CHUNK, BAND_WIN)),
        ],
        out_specs=pl.BlockSpec((QBLK, D_MODEL), lambda b, i: (b * nblk + i, 0)),
        out_shape=jax.ShapeDtypeStruct((t, D_MODEL), BF16),
        scratch_shapes=[
            pltpu.VMEM((2 * QBLK, D_MODEL), BF16),
            pltpu.VMEM((2 * QBLK, D_MODEL), BF16),
        ],
        compiler_params=_params(("parallel", "parallel")),
        name="band",
    )(p_main, p_main, p_main, p_main, p_main, bias)


def _merge_kernel(oa_ref, ob_ref, ga_ref, gb_ref, x_ref, wa_ref, wb_ref, wo_ref, o_ref):
    ya = jnp.dot(oa_ref[...], wa_ref[...], preferred_element_type=F32)
    yb = jnp.dot(ob_ref[...], wb_ref[...], preferred_element_type=F32)
    y = _sigmoid(ga_ref[...].astype(F32)) * ya + _sigmoid(gb_ref[...].astype(F32)) * yb
    o_ref[...] = x_ref[...] + jnp.dot(y.astype(BF16), wo_ref[...], preferred_element_type=F32)


def _merge(oa, ob, p_main, x, wa, wb, wo, tm):
    t = x.shape[0]
    row = lambda i: (i, 0)
    return pl.pallas_call(
        _merge_kernel,
        grid=(t // tm,),
        in_specs=[
            pl.BlockSpec((tm, D_MODEL), row),
            pl.BlockSpec((tm, D_MODEL), row),
            pl.BlockSpec((tm, D_MODEL), lambda i: (i, COL_GA)),
            pl.BlockSpec((tm, D_MODEL), lambda i: (i, COL_GB)),
            pl.BlockSpec((tm, D_MODEL), row),
            _const_spec((D_MODEL, D_MODEL)),
            _const_spec((D_MODEL, D_MODEL)),
            _const_spec((D_MODEL, D_MODEL)),
        ],
        out_specs=pl.BlockSpec((tm, D_MODEL), row),
        out_shape=jax.ShapeDtypeStruct((t, D_MODEL), F32),
        compiler_params=_params(("parallel",)),
        name="merge",
    )(oa, ob, p_main, p_main, x, wa, wb, wo)


def _xattn_kernel(x_ref, g_ref, wq_ref, kv_ref, wo_ref, o_ref):
    x = x_ref[...]
    h = _rmsnorm(x, g_ref[...]).astype(BF16)
    q = jnp.dot(h, wq_ref[...], preferred_element_type=F32).astype(BF16)
    scale = M_DH ** -0.5
    outs = []
    for hh in range(M_HEADS):
        kh = kv_ref[:, hh * M_DH:(hh + 1) * M_DH]
        vh = kv_ref[:, D_MODEL + hh * M_DH:D_MODEL + (hh + 1) * M_DH]
        sc = _dot_nt(q[:, hh * M_DH:(hh + 1) * M_DH], kh) * scale
        m = jnp.max(sc, axis=-1, keepdims=True)
        p = jnp.exp(sc - m)
        denom = jnp.sum(p, axis=-1, keepdims=True)
        outs.append((_dot(p, vh) / denom).astype(BF16))
    o = jnp.concatenate(outs, axis=1)
    o_ref[...] = x + jnp.dot(o, wo_ref[...], preferred_element_type=F32)


def _xattn(x, g, wq, kv, wo, tm, seq, mem_len):
    t = x.shape[0]
    per_batch = seq // tm
    return pl.pallas_call(
        _xattn_kernel,
        grid=(t // tm,),
        in_specs=[
            pl.BlockSpec((tm, D_MODEL), lambda i: (i, 0)),
            _const_spec((1, D_MODEL)),
            _const_spec((D_MODEL, D_MODEL)),
            pl.BlockSpec((mem_len, 2 * D_MODEL), lambda i: (i // per_batch, 0)),
            _const_spec((D_MODEL, D_MODEL)),
        ],
        out_specs=pl.BlockSpec((tm, D_MODEL), lambda i: (i, 0)),
        out_shape=jax.ShapeDtypeStruct((t, D_MODEL), F32),
        compiler_params=_params(("parallel",)),
        name="xattn",
    )(x, g, wq, kv, wo)


FF_CHUNK = 256


def _ffn_kernel(x_ref, g_ref, wgu_ref, wd_ref, gf_ref, o_ref, act_ref, *, final_norm):
    x = x_ref[...]
    h = _rmsnorm(x, g_ref[...]).astype(BF16)
    for c in range(D_FF // FF_CHUNK):
        lo = c * FF_CHUNK
        gate = jnp.dot(h, wgu_ref[:, lo:lo + FF_CHUNK], preferred_element_type=F32)
        up = jnp.dot(h, wgu_ref[:, D_FF + lo:D_FF + lo + FF_CHUNK], preferred_element_type=F32)
        act_ref[:, lo:lo + FF_CHUNK] = (gate * _sigmoid(gate) * up).astype(BF16)
    y = x + jnp.dot(act_ref[...], wd_ref[...], preferred_element_type=F32)
    if final_norm:
        y = _rmsnorm(y, gf_ref[...])
    o_ref[...] = y


def _ffn(x, g, wgu, wd, gf, tm, final_norm):
    t = x.shape[0]
    return pl.pallas_call(
        functools.partial(_ffn_kernel, final_norm=final_norm),
        grid=(t // tm,),
        in_specs=[
            pl.BlockSpec((tm, D_MODEL), lambda i: (i, 0)),
            _const_spec((1, D_MODEL)),
            _const_spec((D_MODEL, 2 * D_FF)),
            _const_spec((D_FF, D_MODEL)),
            _const_spec((1, D_MODEL)),
        ],
        out_specs=pl.BlockSpec((tm, D_MODEL), lambda i: (i, 0)),
        out_shape=jax.ShapeDtypeStruct((t, D_MODEL), F32),
        scratch_shapes=[pltpu.VMEM((tm, D_FF), BF16)],
        compiler_params=_params(("parallel",)),
        name="ffn",
    )(x, g, wgu, wd, gf)


def kernel(x, mem, norm_mix, w_in, conv_w, a_log, dt_bias, head_norm, w_a_out, w_b_out,
           rel_bias, w_o, norm_xattn, norm_mem, w_mq, w_mkv, w_mo, norm_ffn, w_gate_up,
           w_down, norm_final):
    batch, seq, d = x.shape
    mem_len = mem.shape[1]
    depth = w_in.shape[0]
    t = batch * seq
    nsteps = seq // DELTA_ROWS
    nblk = seq // QBLK
    tm = min(512, seq)

    xf = x.reshape(t, d)
    memf = mem.reshape(batch * mem_len, d)
    bias = _band_tables(_band_bias(rel_bias))
    ab_lo = 4 * A_W
    ab_hi = ab_lo + 2 * A_HEADS

    def pad_lanes(v):
        return jnp.pad(v.reshape(1, -1), ((0, 0), (0, AB_PAD - v.shape[-1])))

    for l in range(depth):
        w_l = w_in[l]
        w_main = jnp.concatenate([w_l[:, :ab_lo], w_l[:, ab_hi:]], axis=1).astype(BF16)
        w_ab = jnp.pad(w_l[:, ab_lo:ab_hi], ((0, 0), (0, AB_PAD - 2 * A_HEADS))).astype(BF16)
        p_main, ab = _inproj(xf, norm_mix[l].reshape(1, d), w_main, w_ab, conv_w[l], tm, seq)

        alt = ab[:, :A_HEADS].reshape(batch * nsteps, DELTA_ROWS, A_HEADS).transpose(0, 2, 1)
        alt = jnp.concatenate([alt[:, 0::2], alt[:, 1::2]], axis=1)

        def even_odd_col(v):
            return jnp.concatenate([v[0::2], v[1::2]]).reshape(A_HEADS, 1)

        oa = _delta(p_main, ab, alt, pad_lanes(a_log[l]), pad_lanes(dt_bias[l]),
                    even_odd_col(a_log[l]), even_odd_col(dt_bias[l]),
                    head_norm[l].reshape(1, A_DK), batch, nsteps)
        ob = _band(p_main, bias, batch, nblk)
        x1 = _merge(oa, ob, p_main, xf, w_a_out[l].astype(BF16), w_b_out[l].astype(BF16),
                    w_o[l].astype(BF16), tm)

        kv = _norm_matmul(memf, norm_mem[l].reshape(1, d), w_mkv[l].astype(BF16),
                          min(512, batch * mem_len))
        x2 = _xattn(x1, norm_xattn[l].reshape(1, d), w_mq[l].astype(BF16), kv,
                    w_mo[l].astype(BF16), tm, seq, mem_len)

        xf = _ffn(x2, norm_ffn[l].reshape(1, d), w_gate_up[l].astype(BF16),
                  w_down[l].astype(BF16), norm_final.reshape(1, d), tm,
                  final_norm=(l == depth - 1))
    return xf.reshape(batch, seq, d)
```

```python
import functools

import jax
import jax.numpy as jnp
from jax import lax
from jax.experimental import pallas as pl
from jax.experimental.pallas import tpu as pltpu

F32 = jnp.float32
BF16 = jnp.bfloat16

D_MODEL = 1024
CHUNK = 64
EPS = 1e-6
A_HEADS = 8
A_DK = 128
A_W = A_HEADS * A_DK
CONV_K = 4
B_HEADS = 16
B_DH = 64
BAND_CHUNKS = 8
BAND_LEN = (BAND_CHUNKS + 1) * CHUNK
MAX_REL = 256
N_REL = (CHUNK - 1) + MAX_REL + 1
N_REL_PAD = 384
M_HEADS = 4
M_DH = D_MODEL // M_HEADS
D_FF = 2816
AB_PAD = 128
QBLK = BAND_CHUNKS * CHUNK
BAND_WIN = BAND_LEN + CHUNK
BAND_GROUP = 2
LOG2E = 1.4426950408889634
QB_SCALE = (B_DH ** -0.5) * LOG2E

COL_QKV_A, COL_ZA, COL_QB, COL_KB, COL_VB, COL_GA, COL_GB = 0, 3, 4, 5, 6, 7, 8
N_MAIN = 9 * D_MODEL

VMEM_LIMIT = 56 * 1024 * 1024


def _params(sem):
    return pltpu.CompilerParams(dimension_semantics=sem, vmem_limit_bytes=VMEM_LIMIT)


def _const_spec(shape):
    nd = len(shape)
    return pl.BlockSpec(shape, lambda *_: (0,) * nd, pipeline_mode=pl.Buffered(1))


def _layer_spec(shape, l):
    nd = len(shape)
    return pl.BlockSpec((None,) + tuple(shape), lambda *_: (l,) + (0,) * nd,
                        pipeline_mode=pl.Buffered(1))


def _dot(a, b):
    return jnp.dot(a.astype(BF16), b.astype(BF16), preferred_element_type=F32)


def _dot_nt(a, b):
    return lax.dot_general(a.astype(BF16), b.astype(BF16), (((1,), (1,)), ((), ())),
                           preferred_element_type=F32)


def _sigmoid(x):
    return 1.0 / (1.0 + jnp.exp(-x))


def _softplus(x):
    return jnp.maximum(x, 0.0) + jnp.log1p(jnp.exp(-jnp.abs(x)))


def _rmsnorm(x, g):
    return x * lax.rsqrt(jnp.mean(x * x, axis=-1, keepdims=True) + EPS) * g


CONV_COLS = 256


def _inproj_kernel(x_ref, g_ref, w_ref, wab_ref, cw_ref, o_ref, ab_ref, tail_ref, *,
                   tiles_per_seq):
    tm = x_ref.shape[0]

    @pl.when(pl.program_id(0) % tiles_per_seq == 0)
    def _():
        tail_ref[...] = jnp.zeros_like(tail_ref)

    h = _rmsnorm(x_ref[...], g_ref[...]).astype(BF16)
    ab_ref[...] = jnp.dot(h, wab_ref[...], preferred_element_type=F32)
    for lo in range(0, 3 * A_W, CONV_COLS):
        cols = slice(lo, lo + CONV_COLS)
        o = jnp.dot(h, w_ref[:, cols], preferred_element_type=F32)
        xx = jnp.concatenate([tail_ref[:, cols], o], axis=0)
        tail_ref[:, cols] = o[tm - 8:, :]
        acc = o * cw_ref[CONV_K - 1:CONV_K, cols]
        for s in range(1, CONV_K):
            shifted = pltpu.roll(xx, s, axis=0)[8:, :]
            acc = acc + shifted * cw_ref[CONV_K - 1 - s:CONV_K - s, cols]
        o_ref[:, cols] = (acc * _sigmoid(acc)).astype(o_ref.dtype)
    for lo in range(3 * A_W, N_MAIN, D_MODEL):
        cols = slice(lo, lo + D_MODEL)
        o = jnp.dot(h, w_ref[:, cols], preferred_element_type=F32)
        if lo == COL_QB * D_MODEL:
            o = o * QB_SCALE
        o_ref[:, cols] = o.astype(o_ref.dtype)


def _inproj(x, g, w_main, w_ab, conv_w, tm, seq, l):
    t = x.shape[0]
    n = w_main.shape[-1]
    return pl.pallas_call(
        functools.partial(_inproj_kernel, tiles_per_seq=seq // tm),
        grid=(t // tm,),
        in_specs=[
            pl.BlockSpec((tm, D_MODEL), lambda i: (i, 0)),
            _layer_spec((1, D_MODEL), l),
            _layer_spec((D_MODEL, n), l),
            _layer_spec((D_MODEL, AB_PAD), l),
            _layer_spec((CONV_K, 3 * A_W), l),
        ],
        out_specs=[
            pl.BlockSpec((tm, n), lambda i: (i, 0)),
            pl.BlockSpec((tm, AB_PAD), lambda i: (i, 0)),
        ],
        out_shape=[
            jax.ShapeDtypeStruct((t, n), BF16),
            jax.ShapeDtypeStruct((t, AB_PAD), F32),
        ],
        scratch_shapes=[pltpu.VMEM((8, 3 * A_W), F32)],
        compiler_params=_params(("arbitrary",)),
        name="inproj",
    )(x, g, w_main, w_ab, conv_w)


def _norm_matmul_kernel(x_ref, g_ref, w_ref, o_ref):
    h = _rmsnorm(x_ref[...], g_ref[...]).astype(BF16)
    o_ref[...] = jnp.dot(h, w_ref[...], preferred_element_type=F32).astype(o_ref.dtype)


def _norm_matmul(x, g, w, tm, l):
    t = x.shape[0]
    n = w.shape[-1]
    return pl.pallas_call(
        _norm_matmul_kernel,
        grid=(t // tm,),
        in_specs=[
            pl.BlockSpec((tm, D_MODEL), lambda i: (i, 0)),
            _layer_spec((1, D_MODEL), l),
            _layer_spec((D_MODEL, n), l),
        ],
        out_specs=pl.BlockSpec((tm, n), lambda i: (i, 0)),
        out_shape=jax.ShapeDtypeStruct((t, n), BF16),
        compiler_params=_params(("parallel",)),
        name="mem_kv",
    )(x, g, w)


DELTA_CHUNKS = 4
DELTA_ROWS = DELTA_CHUNKS * CHUNK


def _split_hi_lo(x):
    hi = x.astype(BF16)
    lo = (x - hi.astype(F32)).astype(BF16)
    return hi, lo


def _pair_lhs(split):
    hi, lo = split
    return jnp.concatenate([hi, hi, lo], axis=1)


def _pair_rhs(split, lo_half):
    def bd(x):
        return jnp.concatenate([jnp.where(lo_half, x, 0), jnp.where(lo_half, 0, x)], axis=0)

    hi, lo = split
    return jnp.concatenate([bd(hi), bd(lo), bd(hi)], axis=0)


def _pair_mask(split, mask):
    hi, lo = split
    return jnp.where(mask, hi, 0), jnp.where(mask, lo, 0)


def _lhs_split(a, lo_half):
    hi, lo = _split_hi_lo(a)
    s = jnp.where(lo_half, hi, lo)
    return jnp.concatenate([s, s], axis=1)


def _rhs_split(b):
    hi, lo = _split_hi_lo(b)
    return jnp.concatenate([hi, hi, lo, lo], axis=0)


def _mm(a_split, b_split):
    return jnp.dot(a_split, b_split, preferred_element_type=F32)


def _split3(x):
    p1 = x.astype(BF16)
    r1 = x - p1.astype(F32)
    p2 = r1.astype(BF16)
    p3 = (r1 - p2.astype(F32)).astype(BF16)
    return p1, p2, p3


def _cumsum_masks():
    rows = DELTA_ROWS
    rr = lax.broadcasted_iota(jnp.int32, (rows, rows), 0)
    cc = lax.broadcasted_iota(jnp.int32, (rows, rows), 1)
    tril_bd = jnp.logical_and(rr >= cc, (rr >> 6) == (cc >> 6)).astype(BF16)
    rs = lax.broadcasted_iota(jnp.int32, (rows, 2 * rows), 0)
    cs = lax.broadcasted_iota(jnp.int32, (rows, 2 * rows), 1)
    upto = jnp.logical_and((rs >> 6) == (cs >> 7), (rs & 63) <= (cs & 63))
    first = (cs & CHUNK) == 0
    tri_even = jnp.logical_and(upto, first).astype(BF16)
    tri_odd = jnp.logical_and(upto, jnp.logical_not(first)).astype(BF16)
    return (jnp.concatenate([tril_bd] * 3, axis=1), jnp.concatenate([tri_even] * 3, axis=0),
            jnp.concatenate([tri_odd] * 3, axis=0))


def _delta_kernel(qkv_ref, za_ref, ab_ref, alt_ref, alog_r_ref, dt_r_ref, alog_c_ref, dt_c_ref,
                  hn_ref, tril_ref, tri_even_ref, tri_odd_ref, o_ref, s_ref):
    @pl.when(pl.program_id(1) == 0)
    def _():
        s_ref[...] = jnp.zeros_like(s_ref)

    ab = ab_ref[...]
    g_c = -jnp.exp(alog_r_ref[...]) * _softplus(ab + dt_r_ref[...])
    cum_c = jnp.dot(tril_ref[...], jnp.concatenate(_split3(g_c), axis=0),
                    preferred_element_type=F32)
    g_r = -jnp.exp(alog_c_ref[...]) * _softplus(alt_ref[0] + dt_c_ref[...])
    g_r3 = jnp.concatenate(_split3(g_r), axis=1)
    npair = A_HEADS // 2
    cum_rp = (jnp.dot(g_r3, tri_even_ref[...], preferred_element_type=F32)[:npair]
              + jnp.dot(g_r3, tri_odd_ref[...], preferred_element_type=F32)[npair:])
    beta_c = _sigmoid(ab)
    exp_cum = jnp.exp(cum_c)

    row = lax.broadcasted_iota(jnp.int32, (CHUNK, 2 * CHUNK), 0)
    lane = lax.broadcasted_iota(jnp.int32, (CHUNK, 2 * CHUNK), 1)
    col = lane & (CHUNK - 1)
    lo_half = lane < CHUNK
    incl = row >= col
    strict = row > col
    eye = (row == col).astype(F32)
    same16 = (row >> 4) == (col >> 4)
    same32 = (row >> 5) == (col >> 5)
    off16 = jnp.logical_and(same32, jnp.logical_not(same16))

    eye_bf = eye.astype(BF16)
    zero_blk = jnp.zeros((CHUNK, A_DK), F32)

    qn, kn, vv = [], [], []
    for h in range(A_HEADS):
        q = qkv_ref[:, h * A_DK:(h + 1) * A_DK].astype(F32)
        k = qkv_ref[:, A_W + h * A_DK:A_W + (h + 1) * A_DK].astype(F32)
        qn.append(q * lax.rsqrt(jnp.sum(q * q, axis=-1, keepdims=True) + EPS) * (A_DK ** -0.5))
        kn.append(k * lax.rsqrt(jnp.sum(k * k, axis=-1, keepdims=True) + EPS))
        vv.append(qkv_ref[:, 2 * A_W + h * A_DK:2 * A_W + (h + 1) * A_DK].astype(F32))

    lmat, pmat = [], []
    for c in range(DELTA_CHUNKS):
        rsl = slice(c * CHUNK, (c + 1) * CHUNK)
        for i in range(npair):
            a, b = 2 * i, 2 * i + 1
            ka, kb = kn[a][rsl], kn[b][rsl]
            lhs = jnp.concatenate([jnp.concatenate([ka, kb], axis=1),
                                   jnp.concatenate([qn[a][rsl], qn[b][rsl]], axis=1)], axis=0)
            rhs_nt = jnp.concatenate([jnp.concatenate([ka, zero_blk], axis=1),
                                      jnp.concatenate([zero_blk, kb], axis=1)], axis=0)
            prod = lax.dot_general(lhs.astype(BF16), rhs_nt.astype(BF16),
                                   (((1,), (1,)), ((), ())), preferred_element_type=F32)
            cum2 = jnp.where(lo_half, cum_c[rsl, a:a + 1], cum_c[rsl, b:b + 1])
            diff = cum2 - cum_rp[i:i + 1, c * 2 * CHUNK:(c + 1) * 2 * CHUNK]
            decay = jnp.where(incl, jnp.exp(jnp.where(incl, diff, 0.0)), 0.0)
            beta2 = jnp.where(lo_half, beta_c[rsl, A_HEADS + a:A_HEADS + a + 1],
                              beta_c[rsl, A_HEADS + b:A_HEADS + b + 1])
            lmat.append(jnp.where(strict, beta2 * prod[:CHUNK] * decay, 0.0))
            pm = prod[CHUNK:] * decay
            pmat.append((jnp.where(lo_half, pm, 0.0).astype(BF16),
                         jnp.where(lo_half, 0.0, pm).astype(BF16)))

    rhs, kdec, qdec, elast = [], [], [], []
    for c in range(DELTA_CHUNKS):
        rsl = slice(c * CHUNK, (c + 1) * CHUNK)
        for h in range(A_HEADS):
            kc = kn[h][rsl]
            cum = cum_c[rsl, h:h + 1]
            beta = beta_c[rsl, A_HEADS + h:A_HEADS + h + 1]
            eg = exp_cum[rsl, h:h + 1]
            g_last = cum_c[(c + 1) * CHUNK - 1:(c + 1) * CHUNK, h:h + 1]
            rhs.append(_rhs_split(jnp.concatenate([beta * vv[h][rsl], (beta * eg) * kc], axis=1)))
            qdec.append(qn[h][rsl] * eg)
            kdec.append((kc * jnp.exp(g_last - cum)).astype(BF16))
            elast.append(jnp.exp(g_last))

    def mul(xs, ys):
        return [_mm(_pair_lhs(x), _pair_rhs(y, lo_half)) for x, y in zip(xs, ys)]

    def split(ms):
        return [_split_hi_lo(m) for m in ms]

    l_s = split(lmat)
    d1 = [_pair_mask(s, same16) for s in l_s]
    d2f = mul(d1, d1)
    d2 = split(d2f)
    d4 = split(mul(d2, d2))
    r0 = [(eye_bf - hi, -lo) for hi, lo in d1]
    r0f = [eye - jnp.where(same16, m, 0.0) for m in lmat]
    r1f = [r + m for r, m in zip(r0f, mul(r0, d2))]
    d8 = split(mul(d4, d4))
    r1 = split(r1f)
    r2f = [r + m for r, m in zip(r1f, mul(r1, d4))]
    r2 = split(r2f)
    r3f = [r + m for r, m in zip(r2f, mul(r2, d8))]
    r3 = split(r3f)
    t1 = split(mul([_pair_mask(s, off16) for s in l_s], r3))
    r4f = [r - m for r, m in zip(r3f, mul(r3, t1))]
    r4 = split(r4f)
    t2 = split(mul([_pair_mask(s, jnp.logical_not(same32)) for s in l_s], r4))
    r5f = [r - m for r, m in zip(r4f, mul(r4, t2))]

    sol = []
    for idx, t_pair in enumerate(r5f):
        c, i = divmod(idx, npair)
        swapped = pltpu.roll(t_pair, CHUNK, axis=1)
        for h, t_dup in ((2 * i, jnp.where(lo_half, t_pair, swapped)),
                         (2 * i + 1, jnp.where(lo_half, swapped, t_pair))):
            sol.append((c * A_HEADS + h, _mm(_lhs_split(t_dup, lo_half), rhs[c * A_HEADS + h])))
    sol = [s for _, s in sorted(sol, key=lambda e: e[0])]

    lin, s_add, o_add = [], [], []
    for idx, wu in enumerate(sol):
        c, h = divmod(idx, A_HEADS)
        wu_bf = wu.astype(BF16)
        g = lax.dot_general(kdec[idx], wu_bf, (((0,), (0,)), ((), ())),
                            preferred_element_type=F32)
        pm = pmat[c * npair + h // 2][h % 2]
        pw = jnp.dot(pm, jnp.concatenate([wu_bf, wu_bf], axis=0),
                     preferred_element_type=F32)
        lin.append(jnp.concatenate([g[:, A_DK:], qdec[idx] - pw[:, A_DK:]], axis=0).astype(BF16))
        s_add.append(g[:, :A_DK])
        o_add.append(pw[:, :A_DK])

    state = [s_ref[h] for h in range(A_HEADS)]
    for c in range(DELTA_CHUNKS):
        rsl = slice(c * CHUNK, (c + 1) * CHUNK)
        for h in range(A_HEADS):
            i = c * A_HEADS + h
            lanes = slice(h * A_DK, (h + 1) * A_DK)
            y = jnp.dot(lin[i], state[h].astype(BF16), preferred_element_type=F32)
            state[h] = state[h] * elast[i] - y[:A_DK] + s_add[i]
            o = y[A_DK:] + o_add[i]
            on = _rmsnorm(o, hn_ref[...])
            z = za_ref[rsl, lanes].astype(F32)
            o_ref[rsl, lanes] = (on * (z * _sigmoid(z))).astype(o_ref.dtype)
    for h in range(A_HEADS):
        s_ref[h] = state[h]


def _delta(p_main, ab, alt, alog_r, dt_r, alog_c, dt_c, head_norm, masks, batch, nc):
    t = p_main.shape[0]
    rows = DELTA_ROWS
    tril3, tri_even3, tri_odd3 = masks
    return pl.pallas_call(
        _delta_kernel,
        grid=(batch, nc),
        in_specs=[
            pl.BlockSpec((rows, 3 * A_W), lambda b, n: (b * nc + n, COL_QKV_A)),
            pl.BlockSpec((rows, A_W), lambda b, n: (b * nc + n, COL_ZA)),
            pl.BlockSpec((rows, AB_PAD), lambda b, n: (b * nc + n, 0)),
            pl.BlockSpec((1, A_HEADS, rows), lambda b, n: (b * nc + n, 0, 0)),
            _const_spec((1, AB_PAD)),
            _const_spec((1, AB_PAD)),
            _const_spec((A_HEADS, 1)),
            _const_spec((A_HEADS, 1)),
            _const_spec((1, A_DK)),
            _const_spec(tril3.shape),
            _const_spec(tri_even3.shape),
            _const_spec(tri_odd3.shape),
        ],
        out_specs=pl.BlockSpec((rows, A_W), lambda b, n: (b * nc + n, 0)),
        out_shape=jax.ShapeDtypeStruct((t, A_W), BF16),
        scratch_shapes=[pltpu.VMEM((A_HEADS, A_DK, A_DK), F32)],
        compiler_params=_params(("parallel", "arbitrary")),
        name="delta",
    )(p_main, p_main, ab, alt, alog_r, dt_r, alog_c, dt_c, head_norm, tril3, tri_even3, tri_odd3)


def _band_bias_kernel(win_ref, o_ref):
    rows = o_ref.shape[0] // CHUNK
    shape = (rows, CHUNK, 2 * CHUNK)
    i = lax.broadcasted_iota(jnp.int32, shape, 0) + pl.program_id(0) * rows
    j = lax.broadcasted_iota(jnp.int32, shape, 1)
    cand = lax.broadcasted_iota(jnp.int32, shape, 2)
    onehot = (i - j + (CHUNK - 1) == cand).astype(BF16).reshape(rows * CHUNK, 2 * CHUNK)
    acc = None
    for piece in _split3(win_ref[...]):
        part = jnp.dot(onehot, piece, preferred_element_type=F32)
        acc = part if acc is None else acc + part
    o_ref[...] = acc * LOG2E


def _band_bias(rel_bias):
    nslot = BAND_CHUNKS + 1
    relt = jnp.pad(rel_bias.T, ((0, nslot * CHUNK + CHUNK - N_REL), (0, 0)), mode="edge")
    win = jnp.concatenate([relt[(BAND_CHUNKS - w) * CHUNK:(BAND_CHUNKS - w + 2) * CHUNK]
                           for w in range(nslot)], axis=1)
    ncol = 2 * 2 * CHUNK
    win = jnp.pad(win, ((0, 0), (0, ncol - nslot * B_HEADS)))
    rows = 16
    out = pl.pallas_call(
        _band_bias_kernel,
        grid=(CHUNK // rows,),
        in_specs=[_const_spec((2 * CHUNK, ncol))],
        out_specs=pl.BlockSpec((rows * CHUNK, ncol), lambda g: (g, 0)),
        out_shape=jax.ShapeDtypeStruct((CHUNK * CHUNK, ncol), F32),
        compiler_params=_params(("parallel",)),
        name="band_bias",
    )(win)
    out = out[:, :nslot * B_HEADS].reshape(CHUNK, CHUNK, nslot, B_HEADS)
    return out.transpose(3, 0, 2, 1).reshape(B_HEADS, CHUNK, BAND_LEN)


def _band_kernel(q_ref, kp_ref, kc_ref, vp_ref, vc_ref, bias_ref, o_ref, kwin, vwin):
    blk = pl.program_id(1)
    kwin[:QBLK, :] = kp_ref[...]
    kwin[QBLK:, :] = kc_ref[...]
    vwin[:QBLK, :] = vp_ref[...]
    vwin[QBLK:, :] = vc_ref[...]
    col_chunk = lax.broadcasted_iota(jnp.int32, (4 * CHUNK, BAND_WIN), 1) >> 6
    lane = lax.broadcasted_iota(jnp.int32, (CHUNK, 2 * B_DH), 1)
    even = lane < B_DH
    pairs = B_HEADS // 2

    def body(a2, carry, masked):
        r0 = pl.multiple_of(a2 * 2 * CHUNK, 2 * CHUNK)
        for g0 in range(0, pairs, BAND_GROUP):
            sc = []
            for pr in range(g0, g0 + BAND_GROUP):
                lanes = slice(pr * 2 * B_DH, (pr + 1) * 2 * B_DH)
                qa = q_ref[pl.ds(r0, 2 * CHUNK), lanes]
                q4 = jnp.concatenate(
                    [jnp.where(even, qa[:CHUNK], 0), jnp.where(even, 0, qa[:CHUNK]),
                     jnp.where(even, qa[CHUNK:], 0), jnp.where(even, 0, qa[CHUNK:])], axis=0)
                kp = kwin[pl.ds(r0, BAND_WIN), lanes]
                s = lax.dot_general(q4, kp, (((1,), (1,)), ((), ())),
                                    preferred_element_type=F32) + bias_ref[pr]
                if masked:
                    s = jnp.where(col_chunk + 2 * a2 >= BAND_CHUNKS, s, -1e30)
                sc.append(s)
            m = [jnp.max(s, axis=-1, keepdims=True) for s in sc]
            p = [jnp.exp2(s - mm) for s, mm in zip(sc, m)]
            denom = [jnp.sum(pp, axis=-1, keepdims=True) for pp in p]
            for i, pr in enumerate(range(g0, g0 + BAND_GROUP)):
                lanes = slice(pr * 2 * B_DH, (pr + 1) * 2 * B_DH)
                vp = vwin[pl.ds(r0, BAND_WIN), lanes]
                o4 = jnp.dot(p[i].astype(BF16), vp, preferred_element_type=F32) / denom[i]
                o2 = jnp.concatenate(
                    [jnp.where(even, o4[:CHUNK], o4[CHUNK:2 * CHUNK]),
                     jnp.where(even, o4[2 * CHUNK:3 * CHUNK], o4[3 * CHUNK:])], axis=0)
                o_ref[pl.ds(r0, 2 * CHUNK), lanes] = o2.astype(o_ref.dtype)
        return carry

    steps = QBLK // (2 * CHUNK)

    @pl.when(blk == 0)
    def _():
        lax.fori_loop(0, steps, functools.partial(body, masked=True), 0)

    @pl.when(blk > 0)
    def _():
        lax.fori_loop(0, steps, functools.partial(body, masked=False), 0)


def _band_tables(bias):
    neg = jnp.full((B_HEADS, CHUNK, CHUNK), -1e30, F32)
    first = jnp.concatenate([bias, neg], axis=2)
    second = jnp.concatenate([neg, bias], axis=2)
    t = jnp.stack([first[0::2], first[1::2], second[0::2], second[1::2]], axis=1)
    return t.reshape(B_HEADS // 2, 4 * CHUNK, BAND_WIN)


def _band(p_main, bias, batch, nblk):
    t = p_main.shape[0]

    def prev(b, i):
        return b * nblk + jnp.maximum(i - 1, 0)

    return pl.pallas_call(
        _band_kernel,
        grid=(batch, nblk),
        in_specs=[
            pl.BlockSpec((QBLK, D_MODEL), lambda b, i: (b * nblk + i, COL_QB)),
            pl.BlockSpec((QBLK, D_MODEL), lambda b, i: (prev(b, i), COL_KB)),
            pl.BlockSpec((QBLK, D_MODEL), lambda b, i: (b * nblk + i, COL_KB)),
            pl.BlockSpec((QBLK, D_MODEL), lambda b, i: (prev(b, i), COL_VB)),
            pl.BlockSpec((QBLK, D_MODEL), lambda b, i: (b * nblk + i, COL_VB)),
            _const_spec((B_HEADS // 2, 4 * CHUNK, BAND_WIN)),
        ],
        out_specs=pl.BlockSpec((QBLK, D_MODEL), lambda b, i: (b * nblk + i, 0)),
        out_shape=jax.ShapeDtypeStruct((t, D_MODEL), BF16),
        scratch_shapes=[
            pltpu.VMEM((2 * QBLK, D_MODEL), BF16),
            pltpu.VMEM((2 * QBLK, D_MODEL), BF16),
        ],
        compiler_params=_params(("parallel", "parallel")),
        name="band",
    )(p_main, p_main, p_main, p_main, p_main, bias)


def _merge_kernel(oa_ref, ob_ref, ga_ref, gb_ref, x_ref, wa_ref, wb_ref, wo_ref, o_ref):
    ya = jnp.dot(oa_ref[...], wa_ref[...], preferred_element_type=F32)
    yb = jnp.dot(ob_ref[...], wb_ref[...], preferred_element_type=F32)
    y = _sigmoid(ga_ref[...].astype(F32)) * ya + _sigmoid(gb_ref[...].astype(F32)) * yb
    o_ref[...] = x_ref[...] + jnp.dot(y.astype(BF16), wo_ref[...], preferred_element_type=F32)


def _merge(oa, ob, p_main, x, wa, wb, wo, tm, l):
    t = x.shape[0]
    row = lambda i: (i, 0)
    return pl.pallas_call(
        _merge_kernel,
        grid=(t // tm,),
        in_specs=[
            pl.BlockSpec((tm, D_MODEL), row),
            pl.BlockSpec((tm, D_MODEL), row),
            pl.BlockSpec((tm, D_MODEL), lambda i: (i, COL_GA)),
            pl.BlockSpec((tm, D_MODEL), lambda i: (i, COL_GB)),
            pl.BlockSpec((tm, D_MODEL), row),
            _layer_spec((D_MODEL, D_MODEL), l),
            _layer_spec((D_MODEL, D_MODEL), l),
            _layer_spec((D_MODEL, D_MODEL), l),
        ],
        out_specs=pl.BlockSpec((tm, D_MODEL), row),
        out_shape=jax.ShapeDtypeStruct((t, D_MODEL), F32),
        compiler_params=_params(("parallel",)),
        name="merge",
    )(oa, ob, p_main, p_main, x, wa, wb, wo)


def _xattn_kernel(x_ref, g_ref, wq_ref, kv_ref, wo_ref, o_ref):
    x = x_ref[...]
    h = _rmsnorm(x, g_ref[...]).astype(BF16)
    q = jnp.dot(h, wq_ref[...], preferred_element_type=F32).astype(BF16)
    scale = M_DH ** -0.5
    outs = []
    for hh in range(M_HEADS):
        kh = kv_ref[:, hh * M_DH:(hh + 1) * M_DH]
        vh = kv_ref[:, D_MODEL + hh * M_DH:D_MODEL + (hh + 1) * M_DH]
        sc = _dot_nt(q[:, hh * M_DH:(hh + 1) * M_DH], kh) * scale
        m = jnp.max(sc, axis=-1, keepdims=True)
        p = jnp.exp(sc - m)
        denom = jnp.sum(p, axis=-1, keepdims=True)
        outs.append((_dot(p, vh) / denom).astype(BF16))
    o = jnp.concatenate(outs, axis=1)
    o_ref[...] = x + jnp.dot(o, wo_ref[...], preferred_element_type=F32)


def _xattn(x, g, wq, kv, wo, tm, seq, mem_len, l):
    t = x.shape[0]
    per_batch = seq // tm
    return pl.pallas_call(
        _xattn_kernel,
        grid=(t // tm,),
        in_specs=[
            pl.BlockSpec((tm, D_MODEL), lambda i: (i, 0)),
            _layer_spec((1, D_MODEL), l),
            _layer_spec((D_MODEL, D_MODEL), l),
            pl.BlockSpec((mem_len, 2 * D_MODEL), lambda i: (i // per_batch, 0)),
            _layer_spec((D_MODEL, D_MODEL), l),
        ],
        out_specs=pl.BlockSpec((tm, D_MODEL), lambda i: (i, 0)),
        out_shape=jax.ShapeDtypeStruct((t, D_MODEL), F32),
        compiler_params=_params(("parallel",)),
        name="xattn",
    )(x, g, wq, kv, wo)


FF_CHUNK = 256


def _ffn_kernel(x_ref, g_ref, wgu_ref, wd_ref, gf_ref, o_ref, act_ref, *, final_norm):
    x = x_ref[...]
    h = _rmsnorm(x, g_ref[...]).astype(BF16)
    for c in range(D_FF // FF_CHUNK):
        lo = c * FF_CHUNK
        gate = jnp.dot(h, wgu_ref[:, lo:lo + FF_CHUNK], preferred_element_type=F32)
        up = jnp.dot(h, wgu_ref[:, D_FF + lo:D_FF + lo + FF_CHUNK], preferred_element_type=F32)
        act_ref[:, lo:lo + FF_CHUNK] = (gate * _sigmoid(gate) * up).astype(BF16)
    y = x + jnp.dot(act_ref[...], wd_ref[...], preferred_element_type=F32)
    if final_norm:
        y = _rmsnorm(y, gf_ref[...])
    o_ref[...] = y


def _ffn(x, g, wgu, wd, gf, tm, final_norm, l):
    t = x.shape[0]
    return pl.pallas_call(
        functools.partial(_ffn_kernel, final_norm=final_norm),
        grid=(t // tm,),
        in_specs=[
            pl.BlockSpec((tm, D_MODEL), lambda i: (i, 0)),
            _layer_spec((1, D_MODEL), l),
            _layer_spec((D_MODEL, 2 * D_FF), l),
            _layer_spec((D_FF, D_MODEL), l),
            _const_spec((1, D_MODEL)),
        ],
        out_specs=pl.BlockSpec((tm, D_MODEL), lambda i: (i, 0)),
        out_shape=jax.ShapeDtypeStruct((t, D_MODEL), F32),
        scratch_shapes=[pltpu.VMEM((tm, D_FF), BF16)],
        compiler_params=_params(("parallel",)),
        name="ffn",
    )(x, g, wgu, wd, gf)


def kernel(x, mem, norm_mix, w_in, conv_w, a_log, dt_bias, head_norm, w_a_out, w_b_out,
           rel_bias, w_o, norm_xattn, norm_mem, w_mq, w_mkv, w_mo, norm_ffn, w_gate_up,
           w_down, norm_final):
    batch, seq, d = x.shape
    mem_len = mem.shape[1]
    depth = w_in.shape[0]
    t = batch * seq
    nsteps = seq // DELTA_ROWS
    nblk = seq // QBLK
    tm = min(512, seq)

    xf = x.reshape(t, d)
    memf = mem.reshape(batch * mem_len, d)
    bias = _band_tables(_band_bias(rel_bias))
    masks = _cumsum_masks()

    ab_lo = 4 * A_W
    ab_hi = ab_lo + 2 * A_HEADS
    w_main = jnp.concatenate([w_in[:, :, :ab_lo], w_in[:, :, ab_hi:]], axis=2).astype(BF16)
    w_ab = jnp.pad(w_in[:, :, ab_lo:ab_hi],
                   ((0, 0), (0, 0), (0, AB_PAD - 2 * A_HEADS))).astype(BF16)
    wa, wb, wo = w_a_out.astype(BF16), w_b_out.astype(BF16), w_o.astype(BF16)
    wmq, wmkv, wmo = w_mq.astype(BF16), w_mkv.astype(BF16), w_mo.astype(BF16)
    wgu, wd = w_gate_up.astype(BF16), w_down.astype(BF16)

    def rows3(v):
        return v.reshape(depth, 1, -1)

    g_mix, g_x, g_mem, g_ffn = rows3(norm_mix), rows3(norm_xattn), rows3(norm_mem), rows3(norm_ffn)

    def pad_lanes(v):
        return jnp.pad(v.reshape(1, -1), ((0, 0), (0, AB_PAD - v.shape[-1])))

    def even_odd_col(v):
        return jnp.concatenate([v[0::2], v[1::2]]).reshape(A_HEADS, 1)

    for l in range(depth):
        p_main, ab = _inproj(xf, g_mix, w_main, w_ab, conv_w, tm, seq, l)

        alt = ab[:, :A_HEADS].reshape(batch * nsteps, DELTA_ROWS, A_HEADS).transpose(0, 2, 1)
        alt = jnp.concatenate([alt[:, 0::2], alt[:, 1::2]], axis=1)
        oa = _delta(p_main, ab, alt, pad_lanes(a_log[l]), pad_lanes(dt_bias[l]),
                    even_odd_col(a_log[l]), even_odd_col(dt_bias[l]),
                    head_norm[l].reshape(1, A_DK), masks, batch, nsteps)
        ob = _band(p_main, bias, batch, nblk)
        x1 = _merge(oa, ob, p_main, xf, wa, wb, wo, tm, l)

        kv = _norm_matmul(memf, g_mem, wmkv, min(512, batch * mem_len), l)
        x2 = _xattn(x1, g_x, wmq, kv, wmo, tm, seq, mem_len, l)

        xf = _ffn(x2, g_ffn, wgu, wd, norm_final.reshape(1, d), tm,
                  final_norm=(l == depth - 1), l=l)
    return xf.reshape(batch, seq, d)
```

```python
import functools

import jax
import jax.numpy as jnp
from jax import lax
from jax.experimental import pallas as pl
from jax.experimental.pallas import tpu as pltpu

F32 = jnp.float32
BF16 = jnp.bfloat16

D_MODEL = 1024
CHUNK = 64
EPS = 1e-6
A_HEADS = 8
A_DK = 128
A_W = A_HEADS * A_DK
CONV_K = 4
B_HEADS = 16
B_DH = 64
BAND_CHUNKS = 8
BAND_LEN = (BAND_CHUNKS + 1) * CHUNK
MAX_REL = 256
N_REL = (CHUNK - 1) + MAX_REL + 1
M_HEADS = 4
M_DH = D_MODEL // M_HEADS
D_FF = 2816
AB_PAD = 128
QBLK = BAND_CHUNKS * CHUNK
BAND_WIN = BAND_LEN + CHUNK
BAND_GROUP = 2
LOG2E = 1.4426950408889634
QB_SCALE = (B_DH ** -0.5) * LOG2E

COL_QKV_A, COL_ZA, COL_QB, COL_KB, COL_VB, COL_GA, COL_GB = 0, 3, 4, 5, 6, 7, 8
N_MAIN = 9 * D_MODEL

VMEM_LIMIT = 56 * 1024 * 1024


def _params(sem):
    return pltpu.CompilerParams(dimension_semantics=sem, vmem_limit_bytes=VMEM_LIMIT)


def _const_spec(shape):
    nd = len(shape)
    return pl.BlockSpec(shape, lambda *_: (0,) * nd, pipeline_mode=pl.Buffered(1))


def _layer_spec(shape, l):
    nd = len(shape)
    return pl.BlockSpec((None,) + tuple(shape), lambda *_: (l,) + (0,) * nd,
                        pipeline_mode=pl.Buffered(1))


def _dot(a, b):
    return jnp.dot(a.astype(BF16), b.astype(BF16), preferred_element_type=F32)


def _dot_nt(a, b):
    return lax.dot_general(a.astype(BF16), b.astype(BF16), (((1,), (1,)), ((), ())),
                           preferred_element_type=F32)


def _sigmoid(x):
    return 1.0 / (1.0 + jnp.exp(-x))


def _softplus(x):
    return jnp.maximum(x, 0.0) + jnp.log1p(jnp.exp(-jnp.abs(x)))


def _rmsnorm(x, g):
    return x * lax.rsqrt(jnp.mean(x * x, axis=-1, keepdims=True) + EPS) * g


CONV_COLS = 256


def _inproj_kernel(x_ref, g_ref, wlo_ref, whi_ref, wab_ref, cw_ref, o_ref, ab_ref, tail_ref, *,
                   tiles_per_seq):
    tm = x_ref.shape[0]

    @pl.when(pl.program_id(0) % tiles_per_seq == 0)
    def _():
        tail_ref[...] = jnp.zeros_like(tail_ref)

    h = _rmsnorm(x_ref[...], g_ref[...]).astype(BF16)
    ab_ref[...] = jnp.dot(h, wab_ref[...], preferred_element_type=F32)
    for lo in range(0, 3 * A_W, CONV_COLS):
        cols = slice(lo, lo + CONV_COLS)
        o = jnp.dot(h, wlo_ref[:, cols], preferred_element_type=F32)
        xx = jnp.concatenate([tail_ref[:, cols], o], axis=0)
        tail_ref[:, cols] = o[tm - 8:, :]
        acc = o * cw_ref[CONV_K - 1:CONV_K, cols]
        for s in range(1, CONV_K):
            shifted = pltpu.roll(xx, s, axis=0)[8:, :]
            acc = acc + shifted * cw_ref[CONV_K - 1 - s:CONV_K - s, cols]
        o_ref[:, cols] = (acc * _sigmoid(acc)).astype(o_ref.dtype)
    n_lo = wlo_ref.shape[1]
    for lo in range(3 * A_W, N_MAIN, D_MODEL):
        cols = slice(lo, lo + D_MODEL)
        if lo < n_lo:
            w = wlo_ref[:, cols]
        else:
            w = whi_ref[:, lo - n_lo:lo - n_lo + D_MODEL]
        o = jnp.dot(h, w, preferred_element_type=F32)
        if lo == COL_QB * D_MODEL:
            o = o * QB_SCALE
        o_ref[:, cols] = o.astype(o_ref.dtype)


def _inproj(x, g, w_lo, w_hi, w_ab, conv_w, tm, seq, l):
    t = x.shape[0]
    n = w_lo.shape[-1] + w_hi.shape[-1]
    return pl.pallas_call(
        functools.partial(_inproj_kernel, tiles_per_seq=seq // tm),
        grid=(t // tm,),
        in_specs=[
            pl.BlockSpec((tm, D_MODEL), lambda i: (i, 0)),
            _layer_spec((1, D_MODEL), l),
            _layer_spec((D_MODEL, w_lo.shape[-1]), l),
            _layer_spec((D_MODEL, w_hi.shape[-1]), l),
            _layer_spec((D_MODEL, AB_PAD), l),
            _layer_spec((CONV_K, 3 * A_W), l),
        ],
        out_specs=[
            pl.BlockSpec((tm, n), lambda i: (i, 0)),
            pl.BlockSpec((tm, AB_PAD), lambda i: (i, 0)),
        ],
        out_shape=[
            jax.ShapeDtypeStruct((t, n), BF16),
            jax.ShapeDtypeStruct((t, AB_PAD), F32),
        ],
        scratch_shapes=[pltpu.VMEM((8, 3 * A_W), F32)],
        compiler_params=_params(("arbitrary",)),
        name="inproj",
    )(x, g, w_lo, w_hi, w_ab, conv_w)


def _norm_matmul_kernel(x_ref, g_ref, w_ref, o_ref):
    h = _rmsnorm(x_ref[...], g_ref[...]).astype(BF16)
    o_ref[...] = jnp.dot(h, w_ref[...], preferred_element_type=F32).astype(o_ref.dtype)


def _norm_matmul(x, g, w, tm, l):
    t = x.shape[0]
    n = w.shape[-1]
    return pl.pallas_call(
        _norm_matmul_kernel,
        grid=(t // tm,),
        in_specs=[
            pl.BlockSpec((tm, D_MODEL), lambda i: (i, 0)),
            _layer_spec((1, D_MODEL), l),
            _layer_spec((D_MODEL, n), l),
        ],
        out_specs=pl.BlockSpec((tm, n), lambda i: (i, 0)),
        out_shape=jax.ShapeDtypeStruct((t, n), BF16),
        compiler_params=_params(("parallel",)),
        name="mem_kv",
    )(x, g, w)


DELTA_CHUNKS = 4
DELTA_ROWS = DELTA_CHUNKS * CHUNK


def _split_hi_lo(x):
    hi = x.astype(BF16)
    lo = (x - hi.astype(F32)).astype(BF16)
    return hi, lo


def _pair_lhs(split):
    hi, lo = split
    return jnp.concatenate([hi, hi, lo], axis=1)


def _pair_rhs(split, lo_half):
    def bd(x):
        return jnp.concatenate([jnp.where(lo_half, x, 0), jnp.where(lo_half, 0, x)], axis=0)

    hi, lo = split
    return jnp.concatenate([bd(hi), bd(lo), bd(hi)], axis=0)


def _pair_mask(split, mask):
    hi, lo = split
    return jnp.where(mask, hi, 0), jnp.where(mask, lo, 0)


def _lhs_split(a, lo_half):
    hi, lo = _split_hi_lo(a)
    s = jnp.where(lo_half, hi, lo)
    return jnp.concatenate([s, s], axis=1)


def _rhs_split(b):
    hi, lo = _split_hi_lo(b)
    return jnp.concatenate([hi, hi, lo, lo], axis=0)


def _mm(a_split, b_split):
    return jnp.dot(a_split, b_split, preferred_element_type=F32)


def _split3(x):
    p1 = x.astype(BF16)
    r1 = x - p1.astype(F32)
    p2 = r1.astype(BF16)
    p3 = (r1 - p2.astype(F32)).astype(BF16)
    return p1, p2, p3


def _cumsum_masks():
    rows = DELTA_ROWS
    rr = lax.broadcasted_iota(jnp.int32, (rows, rows), 0)
    cc = lax.broadcasted_iota(jnp.int32, (rows, rows), 1)
    tril_bd = jnp.logical_and(rr >= cc, (rr >> 6) == (cc >> 6)).astype(BF16)
    rs = lax.broadcasted_iota(jnp.int32, (rows, 2 * rows), 0)
    cs = lax.broadcasted_iota(jnp.int32, (rows, 2 * rows), 1)
    upto = jnp.logical_and((rs >> 6) == (cs >> 7), (rs & 63) <= (cs & 63))
    first = (cs & CHUNK) == 0
    tri_even = jnp.logical_and(upto, first).astype(BF16)
    tri_odd = jnp.logical_and(upto, jnp.logical_not(first)).astype(BF16)
    return (jnp.concatenate([tril_bd] * 3, axis=1), jnp.concatenate([tri_even] * 3, axis=0),
            jnp.concatenate([tri_odd] * 3, axis=0))


def _delta_kernel(qkv_ref, za_ref, ab_ref, alt_ref, alog_r_ref, dt_r_ref, alog_c_ref, dt_c_ref,
                  hn_ref, tril_ref, tri_even_ref, tri_odd_ref, o_ref, s_ref):
    @pl.when(pl.program_id(1) == 0)
    def _():
        s_ref[...] = jnp.zeros_like(s_ref)

    ab = ab_ref[...]
    g_c = -jnp.exp(alog_r_ref[...]) * _softplus(ab + dt_r_ref[...])
    cum_c = jnp.dot(tril_ref[...], jnp.concatenate(_split3(g_c), axis=0),
                    preferred_element_type=F32)
    g_r = -jnp.exp(alog_c_ref[...]) * _softplus(alt_ref[0] + dt_c_ref[...])
    g_r3 = jnp.concatenate(_split3(g_r), axis=1)
    npair = A_HEADS // 2
    cum_rp = (jnp.dot(g_r3, tri_even_ref[...], preferred_element_type=F32)[:npair]
              + jnp.dot(g_r3, tri_odd_ref[...], preferred_element_type=F32)[npair:])
    beta_c = _sigmoid(ab)
    exp_cum = jnp.exp(cum_c)

    row = lax.broadcasted_iota(jnp.int32, (CHUNK, 2 * CHUNK), 0)
    lane = lax.broadcasted_iota(jnp.int32, (CHUNK, 2 * CHUNK), 1)
    col = lane & (CHUNK - 1)
    lo_half = lane < CHUNK
    incl = row >= col
    strict = row > col
    eye = (row == col).astype(F32)
    same16 = (row >> 4) == (col >> 4)
    same32 = (row >> 5) == (col >> 5)
    off16 = jnp.logical_and(same32, jnp.logical_not(same16))

    eye_bf = eye.astype(BF16)
    zero_blk = jnp.zeros((CHUNK, A_DK), F32)

    qn, kn, vv = [], [], []
    for h in range(A_HEADS):
        q = qkv_ref[:, h * A_DK:(h + 1) * A_DK].astype(F32)
        k = qkv_ref[:, A_W + h * A_DK:A_W + (h + 1) * A_DK].astype(F32)
        qn.append(q * lax.rsqrt(jnp.sum(q * q, axis=-1, keepdims=True) + EPS) * (A_DK ** -0.5))
        kn.append(k * lax.rsqrt(jnp.sum(k * k, axis=-1, keepdims=True) + EPS))
        vv.append(qkv_ref[:, 2 * A_W + h * A_DK:2 * A_W + (h + 1) * A_DK].astype(F32))

    lmat, pmat = [], []
    for c in range(DELTA_CHUNKS):
        rsl = slice(c * CHUNK, (c + 1) * CHUNK)
        for i in range(npair):
            a, b = 2 * i, 2 * i + 1
            ka, kb = kn[a][rsl], kn[b][rsl]
            lhs = jnp.concatenate([jnp.concatenate([ka, kb], axis=1),
                                   jnp.concatenate([qn[a][rsl], qn[b][rsl]], axis=1)], axis=0)
            rhs_nt = jnp.concatenate([jnp.concatenate([ka, zero_blk], axis=1),
                                      jnp.concatenate([zero_blk, kb], axis=1)], axis=0)
            prod = lax.dot_general(lhs.astype(BF16), rhs_nt.astype(BF16),
                                   (((1,), (1,)), ((), ())), preferred_element_type=F32)
            cum2 = jnp.where(lo_half, cum_c[rsl, a:a + 1], cum_c[rsl, b:b + 1])
            diff = cum2 - cum_rp[i:i + 1, c * 2 * CHUNK:(c + 1) * 2 * CHUNK]
            decay = jnp.where(incl, jnp.exp(jnp.where(incl, diff, 0.0)), 0.0)
            beta2 = jnp.where(lo_half, beta_c[rsl, A_HEADS + a:A_HEADS + a + 1],
                              beta_c[rsl, A_HEADS + b:A_HEADS + b + 1])
            lmat.append(jnp.where(strict, beta2 * prod[:CHUNK] * decay, 0.0))
            pm = prod[CHUNK:] * decay
            pmat.append((jnp.where(lo_half, pm, 0.0).astype(BF16),
                         jnp.where(lo_half, 0.0, pm).astype(BF16)))

    rhs, kdec, qdec, elast = [], [], [], []
    for c in range(DELTA_CHUNKS):
        rsl = slice(c * CHUNK, (c + 1) * CHUNK)
        for h in range(A_HEADS):
            kc = kn[h][rsl]
            cum = cum_c[rsl, h:h + 1]
            beta = beta_c[rsl, A_HEADS + h:A_HEADS + h + 1]
            eg = exp_cum[rsl, h:h + 1]
            g_last = cum_c[(c + 1) * CHUNK - 1:(c + 1) * CHUNK, h:h + 1]
            rhs.append(_rhs_split(jnp.concatenate([beta * vv[h][rsl], (beta * eg) * kc], axis=1)))
            qdec.append(qn[h][rsl] * eg)
            kdec.append((kc * jnp.exp(g_last - cum)).astype(BF16))
            elast.append(jnp.exp(g_last))

    def mul(xs, ys):
        return [_mm(_pair_lhs(x), _pair_rhs(y, lo_half)) for x, y in zip(xs, ys)]

    def split(ms):
        return [_split_hi_lo(m) for m in ms]

    l_s = split(lmat)
    d1 = [_pair_mask(s, same16) for s in l_s]
    d2f = mul(d1, d1)
    d2 = split(d2f)
    d4 = split(mul(d2, d2))
    r0 = [(eye_bf - hi, -lo) for hi, lo in d1]
    r0f = [eye - jnp.where(same16, m, 0.0) for m in lmat]
    r1f = [r + m for r, m in zip(r0f, mul(r0, d2))]
    d8 = split(mul(d4, d4))
    r1 = split(r1f)
    r2f = [r + m for r, m in zip(r1f, mul(r1, d4))]
    r2 = split(r2f)
    r3f = [r + m for r, m in zip(r2f, mul(r2, d8))]
    r3 = split(r3f)
    t1 = split(mul([_pair_mask(s, off16) for s in l_s], r3))
    r4f = [r - m for r, m in zip(r3f, mul(r3, t1))]
    r4 = split(r4f)
    t2 = split(mul([_pair_mask(s, jnp.logical_not(same32)) for s in l_s], r4))
    r5f = [r - m for r, m in zip(r4f, mul(r4, t2))]

    sol = []
    for idx, t_pair in enumerate(r5f):
        c, i = divmod(idx, npair)
        swapped = pltpu.roll(t_pair, CHUNK, axis=1)
        for h, t_dup in ((2 * i, jnp.where(lo_half, t_pair, swapped)),
                         (2 * i + 1, jnp.where(lo_half, swapped, t_pair))):
            sol.append((c * A_HEADS + h, _mm(_lhs_split(t_dup, lo_half), rhs[c * A_HEADS + h])))
    sol = [s for _, s in sorted(sol, key=lambda e: e[0])]

    lin, s_add, o_add = [], [], []
    for idx, wu in enumerate(sol):
        c, h = divmod(idx, A_HEADS)
        wu_bf = wu.astype(BF16)
        g = lax.dot_general(kdec[idx], wu_bf, (((0,), (0,)), ((), ())),
                            preferred_element_type=F32)
        pm = pmat[c * npair + h // 2][h % 2]
        pw = jnp.dot(pm, jnp.concatenate([wu_bf, wu_bf], axis=0),
                     preferred_element_type=F32)
        lin.append(jnp.concatenate([g[:, A_DK:], qdec[idx] - pw[:, A_DK:]], axis=0).astype(BF16))
        s_add.append(g[:, :A_DK])
        o_add.append(pw[:, :A_DK])

    state = [s_ref[h] for h in range(A_HEADS)]
    for c in range(DELTA_CHUNKS):
        rsl = slice(c * CHUNK, (c + 1) * CHUNK)
        for h in range(A_HEADS):
            i = c * A_HEADS + h
            lanes = slice(h * A_DK, (h + 1) * A_DK)
            y = jnp.dot(lin[i], state[h].astype(BF16), preferred_element_type=F32)
            state[h] = state[h] * elast[i] - y[:A_DK] + s_add[i]
            o = y[A_DK:] + o_add[i]
            on = _rmsnorm(o, hn_ref[...])
            z = za_ref[rsl, lanes].astype(F32)
            o_ref[rsl, lanes] = (on * (z * _sigmoid(z))).astype(o_ref.dtype)
    for h in range(A_HEADS):
        s_ref[h] = state[h]


def _delta(p_main, ab, alt, alog_r, dt_r, alog_c, dt_c, head_norm, masks, batch, nc):
    t = p_main.shape[0]
    rows = DELTA_ROWS
    tril3, tri_even3, tri_odd3 = masks
    return pl.pallas_call(
        _delta_kernel,
        grid=(batch, nc),
        in_specs=[
            pl.BlockSpec((rows, 3 * A_W), lambda b, n: (b * nc + n, COL_QKV_A)),
            pl.BlockSpec((rows, A_W), lambda b, n: (b * nc + n, COL_ZA)),
            pl.BlockSpec((rows, AB_PAD), lambda b, n: (b * nc + n, 0)),
            pl.BlockSpec((1, A_HEADS, rows), lambda b, n: (b * nc + n, 0, 0)),
            _const_spec((1, AB_PAD)),
            _const_spec((1, AB_PAD)),
            _const_spec((A_HEADS, 1)),
            _const_spec((A_HEADS, 1)),
            _const_spec((1, A_DK)),
            _const_spec(tril3.shape),
            _const_spec(tri_even3.shape),
            _const_spec(tri_odd3.shape),
        ],
        out_specs=pl.BlockSpec((rows, A_W), lambda b, n: (b * nc + n, 0)),
        out_shape=jax.ShapeDtypeStruct((t, A_W), BF16),
        scratch_shapes=[pltpu.VMEM((A_HEADS, A_DK, A_DK), F32)],
        compiler_params=_params(("parallel", "arbitrary")),
        name="delta",
    )(p_main, p_main, ab, alt, alog_r, dt_r, alog_c, dt_c, head_norm, tril3, tri_even3, tri_odd3)


def _band_bias_kernel(win_ref, o_ref):
    rows = o_ref.shape[0] // CHUNK
    shape = (rows, CHUNK, 2 * CHUNK)
    i = lax.broadcasted_iota(jnp.int32, shape, 0) + pl.program_id(0) * rows
    j = lax.broadcasted_iota(jnp.int32, shape, 1)
    cand = lax.broadcasted_iota(jnp.int32, shape, 2)
    onehot = (i - j + (CHUNK - 1) == cand).astype(BF16).reshape(rows * CHUNK, 2 * CHUNK)
    acc = None
    for piece in _split3(win_ref[...]):
        part = jnp.dot(onehot, piece, preferred_element_type=F32)
        acc = part if acc is None else acc + part
    o_ref[...] = acc * LOG2E


def _band_bias(rel_bias):
    nslot = BAND_CHUNKS + 1
    relt = jnp.pad(rel_bias.T, ((0, nslot * CHUNK + CHUNK - N_REL), (0, 0)), mode="edge")
    win = jnp.concatenate([relt[(BAND_CHUNKS - w) * CHUNK:(BAND_CHUNKS - w + 2) * CHUNK]
                           for w in range(nslot)], axis=1)
    ncol = 2 * 2 * CHUNK
    win = jnp.pad(win, ((0, 0), (0, ncol - nslot * B_HEADS)))
    rows = 16
    out = pl.pallas_call(
        _band_bias_kernel,
        grid=(CHUNK // rows,),
        in_specs=[_const_spec((2 * CHUNK, ncol))],
        out_specs=pl.BlockSpec((rows * CHUNK, ncol), lambda g: (g, 0)),
        out_shape=jax.ShapeDtypeStruct((CHUNK * CHUNK, ncol), F32),
        compiler_params=_params(("parallel",)),
        name="band_bias",
    )(win)
    out = out[:, :nslot * B_HEADS].reshape(CHUNK, CHUNK, nslot, B_HEADS)
    return out.transpose(3, 0, 2, 1).reshape(B_HEADS, CHUNK, BAND_LEN)


def _band_kernel(q_ref, kp_ref, kc_ref, vp_ref, vc_ref, bias_ref, o_ref, kwin, vwin):
    blk = pl.program_id(1)
    kwin[:QBLK, :] = kp_ref[...]
    kwin[QBLK:, :] = kc_ref[...]
    vwin[:QBLK, :] = vp_ref[...]
    vwin[QBLK:, :] = vc_ref[...]
    col_chunk = lax.broadcasted_iota(jnp.int32, (4 * CHUNK, BAND_WIN), 1) >> 6
    lane = lax.broadcasted_iota(jnp.int32, (CHUNK, 2 * B_DH), 1)
    even = lane < B_DH
    pairs = B_HEADS // 2

    def body(a2, carry, masked):
        r0 = pl.multiple_of(a2 * 2 * CHUNK, 2 * CHUNK)
        for g0 in range(0, pairs, BAND_GROUP):
            sc = []
            for pr in range(g0, g0 + BAND_GROUP):
                lanes = slice(pr * 2 * B_DH, (pr + 1) * 2 * B_DH)
                qa = q_ref[pl.ds(r0, 2 * CHUNK), lanes]
                q4 = jnp.concatenate(
                    [jnp.where(even, qa[:CHUNK], 0), jnp.where(even, 0, qa[:CHUNK]),
                     jnp.where(even, qa[CHUNK:], 0), jnp.where(even, 0, qa[CHUNK:])], axis=0)
                kp = kwin[pl.ds(r0, BAND_WIN), lanes]
                s = lax.dot_general(q4, kp, (((1,), (1,)), ((), ())),
                                    preferred_element_type=F32) + bias_ref[pr]
                if masked:
                    s = jnp.where(col_chunk + 2 * a2 >= BAND_CHUNKS, s, -1e30)
                sc.append(s)
            m = [jnp.max(s, axis=-1, keepdims=True) for s in sc]
            p = [jnp.exp2(s - mm) for s, mm in zip(sc, m)]
            denom = [jnp.sum(pp, axis=-1, keepdims=True) for pp in p]
            for i, pr in enumerate(range(g0, g0 + BAND_GROUP)):
                lanes = slice(pr * 2 * B_DH, (pr + 1) * 2 * B_DH)
                vp = vwin[pl.ds(r0, BAND_WIN), lanes]
                o4 = jnp.dot(p[i].astype(BF16), vp, preferred_element_type=F32) / denom[i]
                o2 = jnp.concatenate(
                    [jnp.where(even, o4[:CHUNK], o4[CHUNK:2 * CHUNK]),
                     jnp.where(even, o4[2 * CHUNK:3 * CHUNK], o4[3 * CHUNK:])], axis=0)
                o_ref[pl.ds(r0, 2 * CHUNK), lanes] = o2.astype(o_ref.dtype)
        return carry

    steps = QBLK // (2 * CHUNK)

    @pl.when(blk == 0)
    def _():
        lax.fori_loop(0, steps, functools.partial(body, masked=True), 0)

    @pl.when(blk > 0)
    def _():
        lax.fori_loop(0, steps, functools.partial(body, masked=False), 0)


def _band_tables(bias):
    neg = jnp.full((B_HEADS, CHUNK, CHUNK), -1e30, F32)
    first = jnp.concatenate([bias, neg], axis=2)
    second = jnp.concatenate([neg, bias], axis=2)
    t = jnp.stack([first[0::2], first[1::2], second[0::2], second[1::2]], axis=1)
    return t.reshape(B_HEADS // 2, 4 * CHUNK, BAND_WIN)


def _band(p_main, bias, batch, nblk):
    t = p_main.shape[0]

    def prev(b, i):
        return b * nblk + jnp.maximum(i - 1, 0)

    return pl.pallas_call(
        _band_kernel,
        grid=(batch, nblk),
        in_specs=[
            pl.BlockSpec((QBLK, D_MODEL), lambda b, i: (b * nblk + i, COL_QB)),
            pl.BlockSpec((QBLK, D_MODEL), lambda b, i: (prev(b, i), COL_KB)),
            pl.BlockSpec((QBLK, D_MODEL), lambda b, i: (b * nblk + i, COL_KB)),
            pl.BlockSpec((QBLK, D_MODEL), lambda b, i: (prev(b, i), COL_VB)),
            pl.BlockSpec((QBLK, D_MODEL), lambda b, i: (b * nblk + i, COL_VB)),
            _const_spec((B_HEADS // 2, 4 * CHUNK, BAND_WIN)),
        ],
        out_specs=pl.BlockSpec((QBLK, D_MODEL), lambda b, i: (b * nblk + i, 0)),
        out_shape=jax.ShapeDtypeStruct((t, D_MODEL), BF16),
        scratch_shapes=[
            pltpu.VMEM((2 * QBLK, D_MODEL), BF16),
            pltpu.VMEM((2 * QBLK, D_MODEL), BF16),
        ],
        compiler_params=_params(("parallel", "parallel")),
        name="band",
    )(p_main, p_main, p_main, p_main, p_main, bias)


def _merge_kernel(oa_ref, ob_ref, ga_ref, gb_ref, x_ref, wa_ref, wb_ref, wo_ref, o_ref):
    ya = jnp.dot(oa_ref[...], wa_ref[...], preferred_element_type=F32)
    yb = jnp.dot(ob_ref[...], wb_ref[...], preferred_element_type=F32)
    y = _sigmoid(ga_ref[...].astype(F32)) * ya + _sigmoid(gb_ref[...].astype(F32)) * yb
    o_ref[...] = x_ref[...] + jnp.dot(y.astype(BF16), wo_ref[...], preferred_element_type=F32)


def _merge(oa, ob, p_main, x, wa, wb, wo, tm, l):
    t = x.shape[0]
    row = lambda i: (i, 0)
    return pl.pallas_call(
        _merge_kernel,
        grid=(t // tm,),
        in_specs=[
            pl.BlockSpec((tm, D_MODEL), row),
            pl.BlockSpec((tm, D_MODEL), row),
            pl.BlockSpec((tm, D_MODEL), lambda i: (i, COL_GA)),
            pl.BlockSpec((tm, D_MODEL), lambda i: (i, COL_GB)),
            pl.BlockSpec((tm, D_MODEL), row),
            _layer_spec((D_MODEL, D_MODEL), l),
            _layer_spec((D_MODEL, D_MODEL), l),
            _layer_spec((D_MODEL, D_MODEL), l),
        ],
        out_specs=pl.BlockSpec((tm, D_MODEL), row),
        out_shape=jax.ShapeDtypeStruct((t, D_MODEL), F32),
        compiler_params=_params(("parallel",)),
        name="merge",
    )(oa, ob, p_main, p_main, x, wa, wb, wo)


def _xattn_kernel(x_ref, g_ref, wq_ref, kv_ref, wo_ref, o_ref):
    x = x_ref[...]
    h = _rmsnorm(x, g_ref[...]).astype(BF16)
    q = jnp.dot(h, wq_ref[...], preferred_element_type=F32).astype(BF16)
    scale = M_DH ** -0.5
    outs = []
    for hh in range(M_HEADS):
        kh = kv_ref[:, hh * M_DH:(hh + 1) * M_DH]
        vh = kv_ref[:, D_MODEL + hh * M_DH:D_MODEL + (hh + 1) * M_DH]
        sc = _dot_nt(q[:, hh * M_DH:(hh + 1) * M_DH], kh) * scale
        m = jnp.max(sc, axis=-1, keepdims=True)
        p = jnp.exp(sc - m)
        denom = jnp.sum(p, axis=-1, keepdims=True)
        outs.append((_dot(p, vh) / denom).astype(BF16))
    o = jnp.concatenate(outs, axis=1)
    o_ref[...] = x + jnp.dot(o, wo_ref[...], preferred_element_type=F32)


def _xattn(x, g, wq, kv, wo, tm, seq, mem_len, l):
    t = x.shape[0]
    per_batch = seq // tm
    return pl.pallas_call(
        _xattn_kernel,
        grid=(t // tm,),
        in_specs=[
            pl.BlockSpec((tm, D_MODEL), lambda i: (i, 0)),
            _layer_spec((1, D_MODEL), l),
            _layer_spec((D_MODEL, D_MODEL), l),
            pl.BlockSpec((mem_len, 2 * D_MODEL), lambda i: (i // per_batch, 0)),
            _layer_spec((D_MODEL, D_MODEL), l),
        ],
        out_specs=pl.BlockSpec((tm, D_MODEL), lambda i: (i, 0)),
        out_shape=jax.ShapeDtypeStruct((t, D_MODEL), F32),
        compiler_params=_params(("parallel",)),
        name="xattn",
    )(x, g, wq, kv, wo)


FF_CHUNK = 256


def _ffn_kernel(x_ref, g_ref, wgu_ref, wd_ref, gf_ref, o_ref, act_ref, *, final_norm):
    x = x_ref[...]
    h = _rmsnorm(x, g_ref[...]).astype(BF16)
    for c in range(D_FF // FF_CHUNK):
        lo = c * FF_CHUNK
        gate = jnp.dot(h, wgu_ref[:, lo:lo + FF_CHUNK], preferred_element_type=F32)
        up = jnp.dot(h, wgu_ref[:, D_FF + lo:D_FF + lo + FF_CHUNK], preferred_element_type=F32)
        act_ref[:, lo:lo + FF_CHUNK] = (gate * _sigmoid(gate) * up).astype(BF16)
    y = x + jnp.dot(act_ref[...], wd_ref[...], preferred_element_type=F32)
    if final_norm:
        y = _rmsnorm(y, gf_ref[...])
    o_ref[...] = y


def _ffn(x, g, wgu, wd, gf, tm, final_norm, l):
    t = x.shape[0]
    return pl.pallas_call(
        functools.partial(_ffn_kernel, final_norm=final_norm),
        grid=(t // tm,),
        in_specs=[
            pl.BlockSpec((tm, D_MODEL), lambda i: (i, 0)),
            _layer_spec((1, D_MODEL), l),
            _layer_spec((D_MODEL, 2 * D_FF), l),
            _layer_spec((D_FF, D_MODEL), l),
            _const_spec((1, D_MODEL)),
        ],
        out_specs=pl.BlockSpec((tm, D_MODEL), lambda i: (i, 0)),
        out_shape=jax.ShapeDtypeStruct((t, D_MODEL), F32),
        scratch_shapes=[pltpu.VMEM((tm, D_FF), BF16)],
        compiler_params=_params(("parallel",)),
        name="ffn",
    )(x, g, wgu, wd, gf)


def kernel(x, mem, norm_mix, w_in, conv_w, a_log, dt_bias, head_norm, w_a_out, w_b_out,
           rel_bias, w_o, norm_xattn, norm_mem, w_mq, w_mkv, w_mo, norm_ffn, w_gate_up,
           w_down, norm_final):
    batch, seq, d = x.shape
    mem_len = mem.shape[1]
    depth = w_in.shape[0]
    t = batch * seq
    nsteps = seq // DELTA_ROWS
    nblk = seq // QBLK
    tm = min(512, seq)

    xf = x.reshape(t, d)
    memf = mem.reshape(batch * mem_len, d)
    bias = _band_tables(_band_bias(rel_bias))
    masks = _cumsum_masks()

    ab_lo = 4 * A_W
    ab_hi = ab_lo + 2 * A_HEADS
    w_lo, w_hi = w_in[:, :, :ab_lo].astype(BF16), w_in[:, :, ab_hi:].astype(BF16)
    w_ab = jnp.pad(w_in[:, :, ab_lo:ab_hi],
                   ((0, 0), (0, 0), (0, AB_PAD - 2 * A_HEADS))).astype(BF16)
    wa, wb, wo = w_a_out.astype(BF16), w_b_out.astype(BF16), w_o.astype(BF16)
    wmq, wmkv, wmo = w_mq.astype(BF16), w_mkv.astype(BF16), w_mo.astype(BF16)
    wgu, wd = w_gate_up.astype(BF16), w_down.astype(BF16)

    def rows3(v):
        return v.reshape(depth, 1, -1)

    g_mix, g_x, g_mem, g_ffn = rows3(norm_mix), rows3(norm_xattn), rows3(norm_mem), rows3(norm_ffn)

    def pad_lanes(v):
        return jnp.pad(v.reshape(1, -1), ((0, 0), (0, AB_PAD - v.shape[-1])))

    def even_odd_col(v):
        return jnp.concatenate([v[0::2], v[1::2]]).reshape(A_HEADS, 1)

    for l in range(depth):
        p_main, ab = _inproj(xf, g_mix, w_lo, w_hi, w_ab, conv_w, tm, seq, l)

        alt = ab[:, :A_HEADS].reshape(batch * nsteps, DELTA_ROWS, A_HEADS).transpose(0, 2, 1)
        alt = jnp.concatenate([alt[:, 0::2], alt[:, 1::2]], axis=1)
        oa = _delta(p_main, ab, alt, pad_lanes(a_log[l]), pad_lanes(dt_bias[l]),
                    even_odd_col(a_log[l]), even_odd_col(dt_bias[l]),
                    head_norm[l].reshape(1, A_DK), masks, batch, nsteps)
        ob = _band(p_main, bias, batch, nblk)
        x1 = _merge(oa, ob, p_main, xf, wa, wb, wo, tm, l)

        kv = _norm_matmul(memf, g_mem, wmkv, min(512, batch * mem_len), l)
        x2 = _xattn(x1, g_x, wmq, kv, wmo, tm, seq, mem_len, l)

        xf = _ffn(x2, g_ffn, wgu, wd, norm_final.reshape(1, d), tm,
                  final_norm=(l == depth - 1), l=l)
    return xf.reshape(batch, seq, d)
```

```python
import functools

import jax
import jax.numpy as jnp
from jax import lax
from jax.experimental import pallas as pl
from jax.experimental.pallas import tpu as pltpu

F32 = jnp.float32
BF16 = jnp.bfloat16

D_MODEL = 1024
CHUNK = 64
EPS = 1e-6
A_HEADS = 8
A_DK = 128
A_W = A_HEADS * A_DK
CONV_K = 4
B_HEADS = 16
B_DH = 64
BAND_CHUNKS = 8
BAND_LEN = (BAND_CHUNKS + 1) * CHUNK
MAX_REL = 256
N_REL = (CHUNK - 1) + MAX_REL + 1
M_HEADS = 4
M_DH = D_MODEL // M_HEADS
D_FF = 2816
AB_PAD = 128
QBLK = BAND_CHUNKS * CHUNK
BAND_WIN = BAND_LEN + CHUNK
BAND_GROUP = 2
LOG2E = 1.4426950408889634
QB_SCALE = (B_DH ** -0.5) * LOG2E

COL_QKV_A, COL_ZA, COL_QB, COL_KB, COL_VB, COL_GA, COL_GB = 0, 3, 4, 5, 6, 7, 8
N_MAIN = 9 * D_MODEL

VMEM_LIMIT = 56 * 1024 * 1024


def _params(sem):
    return pltpu.CompilerParams(dimension_semantics=sem, vmem_limit_bytes=VMEM_LIMIT)


def _const_spec(shape):
    nd = len(shape)
    return pl.BlockSpec(shape, lambda *_: (0,) * nd, pipeline_mode=pl.Buffered(1))


def _layer_spec(shape, l):
    nd = len(shape)
    return pl.BlockSpec((None,) + tuple(shape), lambda *_: (l,) + (0,) * nd,
                        pipeline_mode=pl.Buffered(1))


def _dot(a, b):
    return jnp.dot(a.astype(BF16), b.astype(BF16), preferred_element_type=F32)


def _dot_nt(a, b):
    return lax.dot_general(a.astype(BF16), b.astype(BF16), (((1,), (1,)), ((), ())),
                           preferred_element_type=F32)


def _sigmoid(x):
    return 1.0 / (1.0 + jnp.exp(-x))


def _softplus(x):
    return jnp.maximum(x, 0.0) + jnp.log1p(jnp.exp(-jnp.abs(x)))


def _rmsnorm(x, g):
    return x * lax.rsqrt(jnp.mean(x * x, axis=-1, keepdims=True) + EPS) * g


CONV_COLS = 256


def _inproj_kernel(x_ref, g_ref, wlo_ref, whi_ref, wab_ref, cw_ref, o_ref, ab_ref, tail_ref, *,
                   tiles_per_seq):
    tm = x_ref.shape[0]

    @pl.when(pl.program_id(0) % tiles_per_seq == 0)
    def _():
        tail_ref[...] = jnp.zeros_like(tail_ref)

    h = _rmsnorm(x_ref[...], g_ref[...]).astype(BF16)
    ab_ref[...] = jnp.dot(h, wab_ref[...], preferred_element_type=F32)
    for lo in range(0, 3 * A_W, CONV_COLS):
        cols = slice(lo, lo + CONV_COLS)
        o = jnp.dot(h, wlo_ref[:, cols], preferred_element_type=F32)
        xx = jnp.concatenate([tail_ref[:, cols], o], axis=0)
        tail_ref[:, cols] = o[tm - 8:, :]
        acc = o * cw_ref[CONV_K - 1:CONV_K, cols]
        for s in range(1, CONV_K):
            shifted = pltpu.roll(xx, s, axis=0)[8:, :]
            acc = acc + shifted * cw_ref[CONV_K - 1 - s:CONV_K - s, cols]
        o_ref[:, cols] = (acc * _sigmoid(acc)).astype(o_ref.dtype)
    n_lo = wlo_ref.shape[1]
    for lo in range(3 * A_W, N_MAIN, D_MODEL):
        cols = slice(lo, lo + D_MODEL)
        if lo < n_lo:
            w = wlo_ref[:, cols]
        else:
            w = whi_ref[:, lo - n_lo:lo - n_lo + D_MODEL]
        o = jnp.dot(h, w, preferred_element_type=F32)
        if lo == COL_QB * D_MODEL:
            o = o * QB_SCALE
        o_ref[:, cols] = o.astype(o_ref.dtype)


def _inproj(x, g, w_lo, w_hi, w_ab, conv_w, tm, seq, l):
    t = x.shape[0]
    n = w_lo.shape[-1] + w_hi.shape[-1]
    return pl.pallas_call(
        functools.partial(_inproj_kernel, tiles_per_seq=seq // tm),
        grid=(t // tm,),
        in_specs=[
            pl.BlockSpec((tm, D_MODEL), lambda i: (i, 0)),
            _layer_spec((1, D_MODEL), l),
            _layer_spec((D_MODEL, w_lo.shape[-1]), l),
            _layer_spec((D_MODEL, w_hi.shape[-1]), l),
            _layer_spec((D_MODEL, AB_PAD), l),
            _layer_spec((CONV_K, 3 * A_W), l),
        ],
        out_specs=[
            pl.BlockSpec((tm, n), lambda i: (i, 0)),
            pl.BlockSpec((tm, AB_PAD), lambda i: (i, 0)),
        ],
        out_shape=[
            jax.ShapeDtypeStruct((t, n), BF16),
            jax.ShapeDtypeStruct((t, AB_PAD), F32),
        ],
        scratch_shapes=[pltpu.VMEM((8, 3 * A_W), F32)],
        compiler_params=_params(("arbitrary",)),
        name="inproj",
    )(x, g, w_lo, w_hi, w_ab, conv_w)


def _win_cast_kernel(w_ref, lo_ref, hi_ref, ab_ref):
    n_lo = lo_ref.shape[-1]
    n_ab = 2 * A_HEADS
    w = w_ref[...]
    lo_ref[...] = w[:, :n_lo].astype(BF16)
    hi_ref[...] = w[:, n_lo + n_ab:].astype(BF16)
    pad = jnp.zeros((w.shape[0], AB_PAD - n_ab), F32)
    ab_ref[...] = jnp.concatenate([w[:, n_lo:n_lo + n_ab], pad], axis=1).astype(BF16)


def _win_cast(w_in, n_lo, rows):
    depth, d, n = w_in.shape
    n_hi = n - n_lo - 2 * A_HEADS
    blk = lambda l, i: (l, i, 0)
    return pl.pallas_call(
        _win_cast_kernel,
        grid=(depth, d // rows),
        in_specs=[pl.BlockSpec((None, rows, n), blk)],
        out_specs=[
            pl.BlockSpec((None, rows, n_lo), blk),
            pl.BlockSpec((None, rows, n_hi), blk),
            pl.BlockSpec((None, rows, AB_PAD), blk),
        ],
        out_shape=[
            jax.ShapeDtypeStruct((depth, d, n_lo), BF16),
            jax.ShapeDtypeStruct((depth, d, n_hi), BF16),
            jax.ShapeDtypeStruct((depth, d, AB_PAD), BF16),
        ],
        compiler_params=_params(("parallel", "parallel")),
        name="win_cast",
    )(w_in)


def _norm_matmul_kernel(x_ref, g_ref, w_ref, o_ref):
    h = _rmsnorm(x_ref[...], g_ref[...]).astype(BF16)
    o_ref[...] = jnp.dot(h, w_ref[...], preferred_element_type=F32).astype(o_ref.dtype)


def _norm_matmul(x, g, w, tm, l):
    t = x.shape[0]
    n = w.shape[-1]
    return pl.pallas_call(
        _norm_matmul_kernel,
        grid=(t // tm,),
        in_specs=[
            pl.BlockSpec((tm, D_MODEL), lambda i: (i, 0)),
            _layer_spec((1, D_MODEL), l),
            _layer_spec((D_MODEL, n), l),
        ],
        out_specs=pl.BlockSpec((tm, n), lambda i: (i, 0)),
        out_shape=jax.ShapeDtypeStruct((t, n), BF16),
        compiler_params=_params(("parallel",)),
        name="mem_kv",
    )(x, g, w)


DELTA_CHUNKS = 4
DELTA_ROWS = DELTA_CHUNKS * CHUNK


def _split_hi_lo(x):
    hi = x.astype(BF16)
    lo = (x - hi.astype(F32)).astype(BF16)
    return hi, lo


def _pair_lhs(split):
    hi, lo = split
    return jnp.concatenate([hi, hi, lo], axis=1)


def _pair_rhs(split, lo_half):
    def bd(x):
        return jnp.concatenate([jnp.where(lo_half, x, 0), jnp.where(lo_half, 0, x)], axis=0)

    hi, lo = split
    return jnp.concatenate([bd(hi), bd(lo), bd(hi)], axis=0)


def _pair_mask(split, mask):
    hi, lo = split
    return jnp.where(mask, hi, 0), jnp.where(mask, lo, 0)


def _lhs_split(a, lo_half):
    hi, lo = _split_hi_lo(a)
    s = jnp.where(lo_half, hi, lo)
    return jnp.concatenate([s, s], axis=1)


def _rhs_split(b):
    hi, lo = _split_hi_lo(b)
    return jnp.concatenate([hi, hi, lo, lo], axis=0)


def _mm(a_split, b_split):
    return jnp.dot(a_split, b_split, preferred_element_type=F32)


def _split3(x):
    p1 = x.astype(BF16)
    r1 = x - p1.astype(F32)
    p2 = r1.astype(BF16)
    p3 = (r1 - p2.astype(F32)).astype(BF16)
    return p1, p2, p3


def _cumsum_masks():
    rows = DELTA_ROWS
    rr = lax.broadcasted_iota(jnp.int32, (rows, rows), 0)
    cc = lax.broadcasted_iota(jnp.int32, (rows, rows), 1)
    tril_bd = jnp.logical_and(rr >= cc, (rr >> 6) == (cc >> 6)).astype(BF16)
    rs = lax.broadcasted_iota(jnp.int32, (rows, 2 * rows), 0)
    cs = lax.broadcasted_iota(jnp.int32, (rows, 2 * rows), 1)
    upto = jnp.logical_and((rs >> 6) == (cs >> 7), (rs & 63) <= (cs & 63))
    first = (cs & CHUNK) == 0
    tri_even = jnp.logical_and(upto, first).astype(BF16)
    tri_odd = jnp.logical_and(upto, jnp.logical_not(first)).astype(BF16)
    return (jnp.concatenate([tril_bd] * 3, axis=1), jnp.concatenate([tri_even] * 3, axis=0),
            jnp.concatenate([tri_odd] * 3, axis=0))


def _delta_kernel(qkv_ref, za_ref, ab_ref, alt_ref, alog_r_ref, dt_r_ref, alog_c_ref, dt_c_ref,
                  hn_ref, tril_ref, tri_even_ref, tri_odd_ref, o_ref, s_ref):
    @pl.when(pl.program_id(1) == 0)
    def _():
        s_ref[...] = jnp.zeros_like(s_ref)

    ab = ab_ref[...]
    g_c = -jnp.exp(alog_r_ref[...]) * _softplus(ab + dt_r_ref[...])
    cum_c = jnp.dot(tril_ref[...], jnp.concatenate(_split3(g_c), axis=0),
                    preferred_element_type=F32)
    g_r = -jnp.exp(alog_c_ref[...]) * _softplus(alt_ref[0] + dt_c_ref[...])
    g_r3 = jnp.concatenate(_split3(g_r), axis=1)
    npair = A_HEADS // 2
    cum_rp = (jnp.dot(g_r3, tri_even_ref[...], preferred_element_type=F32)[:npair]
              + jnp.dot(g_r3, tri_odd_ref[...], preferred_element_type=F32)[npair:])
    beta_c = _sigmoid(ab)

    row = lax.broadcasted_iota(jnp.int32, (CHUNK, 2 * CHUNK), 0)
    lane = lax.broadcasted_iota(jnp.int32, (CHUNK, 2 * CHUNK), 1)
    col = lane & (CHUNK - 1)
    lo_half = lane < CHUNK
    incl = row >= col
    strict = row > col
    eye = (row == col).astype(F32)
    same16 = (row >> 4) == (col >> 4)
    same32 = (row >> 5) == (col >> 5)
    off16 = jnp.logical_and(same32, jnp.logical_not(same16))

    eye_bf = eye.astype(BF16)
    zero_blk = jnp.zeros((CHUNK, A_DK), F32)

    qn, kn, vv = [], [], []
    for h in range(A_HEADS):
        q = qkv_ref[:, h * A_DK:(h + 1) * A_DK].astype(F32)
        k = qkv_ref[:, A_W + h * A_DK:A_W + (h + 1) * A_DK].astype(F32)
        qn.append(q * lax.rsqrt(jnp.sum(q * q, axis=-1, keepdims=True) + EPS) * (A_DK ** -0.5))
        kn.append(k * lax.rsqrt(jnp.sum(k * k, axis=-1, keepdims=True) + EPS))
        vv.append(qkv_ref[:, 2 * A_W + h * A_DK:2 * A_W + (h + 1) * A_DK].astype(F32))

    cum_b, beta_b = {}, {}
    for c in range(DELTA_CHUNKS):
        rsl = slice(c * CHUNK, (c + 1) * CHUNK)
        for h in range(A_HEADS):
            cum_b[c, h] = jnp.broadcast_to(cum_c[rsl, h:h + 1], (CHUNK, A_DK))
            beta_b[c, h] = jnp.broadcast_to(beta_c[rsl, A_HEADS + h:A_HEADS + h + 1],
                                            (CHUNK, A_DK))

    lmat, pmat = [], []
    for c in range(DELTA_CHUNKS):
        rsl = slice(c * CHUNK, (c + 1) * CHUNK)
        for i in range(npair):
            a, b = 2 * i, 2 * i + 1
            ka, kb = kn[a][rsl], kn[b][rsl]
            lhs = jnp.concatenate([jnp.concatenate([ka, kb], axis=1),
                                   jnp.concatenate([qn[a][rsl], qn[b][rsl]], axis=1)], axis=0)
            rhs_nt = jnp.concatenate([jnp.concatenate([ka, zero_blk], axis=1),
                                      jnp.concatenate([zero_blk, kb], axis=1)], axis=0)
            prod = lax.dot_general(lhs.astype(BF16), rhs_nt.astype(BF16),
                                   (((1,), (1,)), ((), ())), preferred_element_type=F32)
            cum2 = jnp.where(lo_half, cum_b[c, a], cum_b[c, b])
            diff = cum2 - cum_rp[i:i + 1, c * 2 * CHUNK:(c + 1) * 2 * CHUNK]
            decay = jnp.where(incl, jnp.exp(jnp.where(incl, diff, 0.0)), 0.0)
            beta2 = jnp.where(lo_half, beta_b[c, a], beta_b[c, b])
            lmat.append(jnp.where(strict, beta2 * prod[:CHUNK] * decay, 0.0))
            pm = prod[CHUNK:] * decay
            pmat.append((jnp.where(lo_half, pm, 0.0).astype(BF16),
                         jnp.where(lo_half, 0.0, pm).astype(BF16)))

    rhs, kdec, qdec, elast = [], [], [], []
    for c in range(DELTA_CHUNKS):
        rsl = slice(c * CHUNK, (c + 1) * CHUNK)
        for h in range(A_HEADS):
            kc = kn[h][rsl]
            cum = cum_b[c, h]
            beta = beta_b[c, h]
            eg = jnp.exp(cum)
            g_last = cum[CHUNK - 1:CHUNK, :]
            rhs.append(_rhs_split(jnp.concatenate([beta * vv[h][rsl], (beta * eg) * kc], axis=1)))
            qdec.append(qn[h][rsl] * eg)
            kdec.append((kc * jnp.exp(g_last - cum)).astype(BF16))
            elast.append(jnp.exp(g_last))

    def mul(xs, ys):
        return [_mm(_pair_lhs(x), _pair_rhs(y, lo_half)) for x, y in zip(xs, ys)]

    def mul2(xs, zs, ys):
        both = [_mm(jnp.concatenate([_pair_lhs(x), _pair_lhs(z)], axis=0), _pair_rhs(y, lo_half))
                for x, z, y in zip(xs, zs, ys)]
        return [b[:CHUNK] for b in both], [b[CHUNK:] for b in both]

    def split(ms):
        return [_split_hi_lo(m) for m in ms]

    l_s = split(lmat)
    d1 = [_pair_mask(s, same16) for s in l_s]
    d2 = split(mul(d1, d1))
    r0 = [(eye_bf - hi, -lo) for hi, lo in d1]
    r0f = [eye - jnp.where(same16, m, 0.0) for m in lmat]
    d4f, r0d2 = mul2(d2, r0, d2)
    d4 = split(d4f)
    r1f = [r + m for r, m in zip(r0f, r0d2)]
    r1 = split(r1f)
    d8f, r1d4 = mul2(d4, r1, d4)
    d8 = split(d8f)
    r2f = [r + m for r, m in zip(r1f, r1d4)]
    r2 = split(r2f)
    r3f = [r + m for r, m in zip(r2f, mul(r2, d8))]
    r3 = split(r3f)
    t1 = split(mul([_pair_mask(s, off16) for s in l_s], r3))
    r4f = [r - m for r, m in zip(r3f, mul(r3, t1))]
    r4 = split(r4f)
    t2 = split(mul([_pair_mask(s, jnp.logical_not(same32)) for s in l_s], r4))
    r5f = [r - m for r, m in zip(r4f, mul(r4, t2))]

    sol = []
    for idx, t_pair in enumerate(r5f):
        c, i = divmod(idx, npair)
        swapped = pltpu.roll(t_pair, CHUNK, axis=1)
        for h, t_dup in ((2 * i, jnp.where(lo_half, t_pair, swapped)),
                         (2 * i + 1, jnp.where(lo_half, swapped, t_pair))):
            sol.append((c * A_HEADS + h, _mm(_lhs_split(t_dup, lo_half), rhs[c * A_HEADS + h])))
    sol = [s for _, s in sorted(sol, key=lambda e: e[0])]

    lin, s_add, o_add = [], [], []
    for idx, wu in enumerate(sol):
        c, h = divmod(idx, A_HEADS)
        wu_bf = wu.astype(BF16)
        g = lax.dot_general(kdec[idx], wu_bf, (((0,), (0,)), ((), ())),
                            preferred_element_type=F32)
        pm = pmat[c * npair + h // 2][h % 2]
        pw = jnp.dot(pm, jnp.concatenate([wu_bf, wu_bf], axis=0),
                     preferred_element_type=F32)
        lin.append(jnp.concatenate([g[:, A_DK:], qdec[idx] - pw[:, A_DK:]], axis=0).astype(BF16))
        s_add.append(g[:, :A_DK])
        o_add.append(pw[:, :A_DK])

    state = [s_ref[h] for h in range(A_HEADS)]
    for c in range(DELTA_CHUNKS):
        rsl = slice(c * CHUNK, (c + 1) * CHUNK)
        for h in range(A_HEADS):
            i = c * A_HEADS + h
            lanes = slice(h * A_DK, (h + 1) * A_DK)
            y = jnp.dot(lin[i], state[h].astype(BF16), preferred_element_type=F32)
            state[h] = state[h] * elast[i] - y[:A_DK] + s_add[i]
            o = y[A_DK:] + o_add[i]
            on = _rmsnorm(o, hn_ref[...])
            z = za_ref[rsl, lanes].astype(F32)
            o_ref[rsl, lanes] = (on * (z * _sigmoid(z))).astype(o_ref.dtype)
    for h in range(A_HEADS):
        s_ref[h] = state[h]


def _delta(p_main, ab, alt, alog_r, dt_r, alog_c, dt_c, head_norm, masks, batch, nc):
    t = p_main.shape[0]
    rows = DELTA_ROWS
    tril3, tri_even3, tri_odd3 = masks
    return pl.pallas_call(
        _delta_kernel,
        grid=(batch, nc),
        in_specs=[
            pl.BlockSpec((rows, 3 * A_W), lambda b, n: (b * nc + n, COL_QKV_A)),
            pl.BlockSpec((rows, A_W), lambda b, n: (b * nc + n, COL_ZA)),
            pl.BlockSpec((rows, AB_PAD), lambda b, n: (b * nc + n, 0)),
            pl.BlockSpec((1, A_HEADS, rows), lambda b, n: (b * nc + n, 0, 0)),
            _const_spec((1, AB_PAD)),
            _const_spec((1, AB_PAD)),
            _const_spec((A_HEADS, 1)),
            _const_spec((A_HEADS, 1)),
            _const_spec((1, A_DK)),
            _const_spec(tril3.shape),
            _const_spec(tri_even3.shape),
            _const_spec(tri_odd3.shape),
        ],
        out_specs=pl.BlockSpec((rows, A_W), lambda b, n: (b * nc + n, 0)),
        out_shape=jax.ShapeDtypeStruct((t, A_W), BF16),
        scratch_shapes=[pltpu.VMEM((A_HEADS, A_DK, A_DK), F32)],
        compiler_params=_params(("parallel", "arbitrary")),
        name="delta",
    )(p_main, p_main, ab, alt, alog_r, dt_r, alog_c, dt_c, head_norm, tril3, tri_even3, tri_odd3)


def _band_bias_kernel(win_ref, o_ref):
    rows = o_ref.shape[0] // CHUNK
    shape = (rows, CHUNK, 2 * CHUNK)
    i = lax.broadcasted_iota(jnp.int32, shape, 0) + pl.program_id(0) * rows
    j = lax.broadcasted_iota(jnp.int32, shape, 1)
    cand = lax.broadcasted_iota(jnp.int32, shape, 2)
    onehot = (i - j + (CHUNK - 1) == cand).astype(BF16).reshape(rows * CHUNK, 2 * CHUNK)
    acc = None
    for piece in _split3(win_ref[...]):
        part = jnp.dot(onehot, piece, preferred_element_type=F32)
        acc = part if acc is None else acc + part
    o_ref[...] = acc * LOG2E


def _band_bias(rel_bias):
    nslot = BAND_CHUNKS + 1
    relt = jnp.pad(rel_bias.T, ((0, nslot * CHUNK + CHUNK - N_REL), (0, 0)), mode="edge")
    win = jnp.concatenate([relt[(BAND_CHUNKS - w) * CHUNK:(BAND_CHUNKS - w + 2) * CHUNK]
                           for w in range(nslot)], axis=1)
    ncol = 2 * 2 * CHUNK
    win = jnp.pad(win, ((0, 0), (0, ncol - nslot * B_HEADS)))
    rows = 16
    out = pl.pallas_call(
        _band_bias_kernel,
        grid=(CHUNK // rows,),
        in_specs=[_const_spec((2 * CHUNK, ncol))],
        out_specs=pl.BlockSpec((rows * CHUNK, ncol), lambda g: (g, 0)),
        out_shape=jax.ShapeDtypeStruct((CHUNK * CHUNK, ncol), F32),
        compiler_params=_params(("parallel",)),
        name="band_bias",
    )(win)
    out = out[:, :nslot * B_HEADS].reshape(CHUNK, CHUNK, nslot, B_HEADS)
    return out.transpose(3, 0, 2, 1).reshape(B_HEADS, CHUNK, BAND_LEN)


def _band_kernel(q_ref, kp_ref, kc_ref, vp_ref, vc_ref, bias_ref, o_ref, kwin, vwin):
    blk = pl.program_id(1)
    kwin[:QBLK, :] = kp_ref[...]
    kwin[QBLK:, :] = kc_ref[...]
    vwin[:QBLK, :] = vp_ref[...]
    vwin[QBLK:, :] = vc_ref[...]
    col_chunk = lax.broadcasted_iota(jnp.int32, (4 * CHUNK, BAND_WIN), 1) >> 6
    lane = lax.broadcasted_iota(jnp.int32, (CHUNK, 2 * B_DH), 1)
    even = lane < B_DH
    pairs = B_HEADS // 2

    def body(a2, carry, masked):
        r0 = pl.multiple_of(a2 * 2 * CHUNK, 2 * CHUNK)
        for g0 in range(0, pairs, BAND_GROUP):
            sc = []
            for pr in range(g0, g0 + BAND_GROUP):
                lanes = slice(pr * 2 * B_DH, (pr + 1) * 2 * B_DH)
                qa = q_ref[pl.ds(r0, 2 * CHUNK), lanes]
                q4 = jnp.concatenate(
                    [jnp.where(even, qa[:CHUNK], 0), jnp.where(even, 0, qa[:CHUNK]),
                     jnp.where(even, qa[CHUNK:], 0), jnp.where(even, 0, qa[CHUNK:])], axis=0)
                kp = kwin[pl.ds(r0, BAND_WIN), lanes]
                s = lax.dot_general(q4, kp, (((1,), (1,)), ((), ())),
                                    preferred_element_type=F32) + bias_ref[pr]
                if masked:
                    s = jnp.where(col_chunk + 2 * a2 >= BAND_CHUNKS, s, -1e30)
                sc.append(s)
            m = [jnp.max(s, axis=-1, keepdims=True) for s in sc]
            p = [jnp.exp2(s - mm) for s, mm in zip(sc, m)]
            denom = [jnp.sum(pp, axis=-1, keepdims=True) for pp in p]
            for i, pr in enumerate(range(g0, g0 + BAND_GROUP)):
                lanes = slice(pr * 2 * B_DH, (pr + 1) * 2 * B_DH)
                vp = vwin[pl.ds(r0, BAND_WIN), lanes]
                o4 = jnp.dot(p[i].astype(BF16), vp, preferred_element_type=F32) / denom[i]
                o2 = jnp.concatenate(
                    [jnp.where(even, o4[:CHUNK], o4[CHUNK:2 * CHUNK]),
                     jnp.where(even, o4[2 * CHUNK:3 * CHUNK], o4[3 * CHUNK:])], axis=0)
                o_ref[pl.ds(r0, 2 * CHUNK), lanes] = o2.astype(o_ref.dtype)
        return carry

    steps = QBLK // (2 * CHUNK)

    @pl.when(blk == 0)
    def _():
        lax.fori_loop(0, steps, functools.partial(body, masked=True), 0)

    @pl.when(blk > 0)
    def _():
        lax.fori_loop(0, steps, functools.partial(body, masked=False), 0)


def _band_tables(bias):
    neg = jnp.full((B_HEADS, CHUNK, CHUNK), -1e30, F32)
    first = jnp.concatenate([bias, neg], axis=2)
    second = jnp.concatenate([neg, bias], axis=2)
    t = jnp.stack([first[0::2], first[1::2], second[0::2], second[1::2]], axis=1)
    return t.reshape(B_HEADS // 2, 4 * CHUNK, BAND_WIN)


def _band(p_main, bias, batch, nblk):
    t = p_main.shape[0]

    def prev(b, i):
        return b * nblk + jnp.maximum(i - 1, 0)

    return pl.pallas_call(
        _band_kernel,
        grid=(batch, nblk),
        in_specs=[
            pl.BlockSpec((QBLK, D_MODEL), lambda b, i: (b * nblk + i, COL_QB)),
            pl.BlockSpec((QBLK, D_MODEL), lambda b, i: (prev(b, i), COL_KB)),
            pl.BlockSpec((QBLK, D_MODEL), lambda b, i: (b * nblk + i, COL_KB)),
            pl.BlockSpec((QBLK, D_MODEL), lambda b, i: (prev(b, i), COL_VB)),
            pl.BlockSpec((QBLK, D_MODEL), lambda b, i: (b * nblk + i, COL_VB)),
            _const_spec((B_HEADS // 2, 4 * CHUNK, BAND_WIN)),
        ],
        out_specs=pl.BlockSpec((QBLK, D_MODEL), lambda b, i: (b * nblk + i, 0)),
        out_shape=jax.ShapeDtypeStruct((t, D_MODEL), BF16),
        scratch_shapes=[
            pltpu.VMEM((2 * QBLK, D_MODEL), BF16),
            pltpu.VMEM((2 * QBLK, D_MODEL), BF16),
        ],
        compiler_params=_params(("parallel", "parallel")),
        name="band",
    )(p_main, p_main, p_main, p_main, p_main, bias)


def _merge_kernel(oa_ref, ob_ref, ga_ref, gb_ref, x_ref, wa_ref, wb_ref, wo_ref, o_ref):
    ya = jnp.dot(oa_ref[...], wa_ref[...], preferred_element_type=F32)
    yb = jnp.dot(ob_ref[...], wb_ref[...], preferred_element_type=F32)
    y = _sigmoid(ga_ref[...].astype(F32)) * ya + _sigmoid(gb_ref[...].astype(F32)) * yb
    o_ref[...] = x_ref[...] + jnp.dot(y.astype(BF16), wo_ref[...], preferred_element_type=F32)


def _merge(oa, ob, p_main, x, wa, wb, wo, tm, l):
    t = x.shape[0]
    row = lambda i: (i, 0)
    return pl.pallas_call(
        _merge_kernel,
        grid=(t // tm,),
        in_specs=[
            pl.BlockSpec((tm, D_MODEL), row),
            pl.BlockSpec((tm, D_MODEL), row),
            pl.BlockSpec((tm, D_MODEL), lambda i: (i, COL_GA)),
            pl.BlockSpec((tm, D_MODEL), lambda i: (i, COL_GB)),
            pl.BlockSpec((tm, D_MODEL), row),
            _layer_spec((D_MODEL, D_MODEL), l),
            _layer_spec((D_MODEL, D_MODEL), l),
            _layer_spec((D_MODEL, D_MODEL), l),
        ],
        out_specs=pl.BlockSpec((tm, D_MODEL), row),
        out_shape=jax.ShapeDtypeStruct((t, D_MODEL), F32),
        compiler_params=_params(("parallel",)),
        name="merge",
    )(oa, ob, p_main, p_main, x, wa, wb, wo)


def _xattn_kernel(x_ref, g_ref, wq_ref, kv_ref, wo_ref, o_ref):
    x = x_ref[...]
    h = _rmsnorm(x, g_ref[...]).astype(BF16)
    q = jnp.dot(h, wq_ref[...], preferred_element_type=F32).astype(BF16)
    scale = M_DH ** -0.5
    outs = []
    for hh in range(M_HEADS):
        kh = kv_ref[:, hh * M_DH:(hh + 1) * M_DH]
        vh = kv_ref[:, D_MODEL + hh * M_DH:D_MODEL + (hh + 1) * M_DH]
        sc = _dot_nt(q[:, hh * M_DH:(hh + 1) * M_DH], kh) * scale
        m = jnp.max(sc, axis=-1, keepdims=True)
        p = jnp.exp(sc - m)
        denom = jnp.sum(p, axis=-1, keepdims=True)
        outs.append((_dot(p, vh) / denom).astype(BF16))
    o = jnp.concatenate(outs, axis=1)
    o_ref[...] = x + jnp.dot(o, wo_ref[...], preferred_element_type=F32)


def _xattn(x, g, wq, kv, wo, tm, seq, mem_len, l):
    t = x.shape[0]
    per_batch = seq // tm
    return pl.pallas_call(
        _xattn_kernel,
        grid=(t // tm,),
        in_specs=[
            pl.BlockSpec((tm, D_MODEL), lambda i: (i, 0)),
            _layer_spec((1, D_MODEL), l),
            _layer_spec((D_MODEL, D_MODEL), l),
            pl.BlockSpec((mem_len, 2 * D_MODEL), lambda i: (i // per_batch, 0)),
            _layer_spec((D_MODEL, D_MODEL), l),
        ],
        out_specs=pl.BlockSpec((tm, D_MODEL), lambda i: (i, 0)),
        out_shape=jax.ShapeDtypeStruct((t, D_MODEL), F32),
        compiler_params=_params(("parallel",)),
        name="xattn",
    )(x, g, wq, kv, wo)


FF_CHUNK = 256


def _ffn_kernel(x_ref, g_ref, wgu_ref, wd_ref, gf_ref, o_ref, act_ref, *, final_norm):
    x = x_ref[...]
    h = _rmsnorm(x, g_ref[...]).astype(BF16)
    for c in range(D_FF // FF_CHUNK):
        lo = c * FF_CHUNK
        gate = jnp.dot(h, wgu_ref[:, lo:lo + FF_CHUNK], preferred_element_type=F32)
        up = jnp.dot(h, wgu_ref[:, D_FF + lo:D_FF + lo + FF_CHUNK], preferred_element_type=F32)
        act_ref[:, lo:lo + FF_CHUNK] = (gate * _sigmoid(gate) * up).astype(BF16)
    y = x + jnp.dot(act_ref[...], wd_ref[...], preferred_element_type=F32)
    if final_norm:
        y = _rmsnorm(y, gf_ref[...])
    o_ref[...] = y


def _ffn(x, g, wgu, wd, gf, tm, final_norm, l):
    t = x.shape[0]
    return pl.pallas_call(
        functools.partial(_ffn_kernel, final_norm=final_norm),
        grid=(t // tm,),
        in_specs=[
            pl.BlockSpec((tm, D_MODEL), lambda i: (i, 0)),
            _layer_spec((1, D_MODEL), l),
            _layer_spec((D_MODEL, 2 * D_FF), l),
            _layer_spec((D_FF, D_MODEL), l),
            _const_spec((1, D_MODEL)),
        ],
        out_specs=pl.BlockSpec((tm, D_MODEL), lambda i: (i, 0)),
        out_shape=jax.ShapeDtypeStruct((t, D_MODEL), F32),
        scratch_shapes=[pltpu.VMEM((tm, D_FF), BF16)],
        compiler_params=_params(("parallel",)),
        name="ffn",
    )(x, g, wgu, wd, gf)


def kernel(x, mem, norm_mix, w_in, conv_w, a_log, dt_bias, head_norm, w_a_out, w_b_out,
           rel_bias, w_o, norm_xattn, norm_mem, w_mq, w_mkv, w_mo, norm_ffn, w_gate_up,
           w_down, norm_final):
    batch, seq, d = x.shape
    mem_len = mem.shape[1]
    depth = w_in.shape[0]
    t = batch * seq
    nsteps = seq // DELTA_ROWS
    nblk = seq // QBLK
    tm = min(512, seq)

    xf = x.reshape(t, d)
    memf = mem.reshape(batch * mem_len, d)
    bias = _band_tables(_band_bias(rel_bias))
    masks = _cumsum_masks()

    w_lo, w_hi, w_ab = _win_cast(w_in, 4 * A_W, 128)
    wa, wb, wo = w_a_out.astype(BF16), w_b_out.astype(BF16), w_o.astype(BF16)
    wmq, wmkv, wmo = w_mq.astype(BF16), w_mkv.astype(BF16), w_mo.astype(BF16)
    wgu, wd = w_gate_up.astype(BF16), w_down.astype(BF16)

    def rows3(v):
        return v.reshape(depth, 1, -1)

    g_mix, g_x, g_mem, g_ffn = rows3(norm_mix), rows3(norm_xattn), rows3(norm_mem), rows3(norm_ffn)

    def pad_lanes(v):
        return jnp.pad(v.reshape(1, -1), ((0, 0), (0, AB_PAD - v.shape[-1])))

    def even_odd_col(v):
        return jnp.concatenate([v[0::2], v[1::2]]).reshape(A_HEADS, 1)

    for l in range(depth):
        p_main, ab = _inproj(xf, g_mix, w_lo, w_hi, w_ab, conv_w, tm, seq, l)

        alt = ab[:, :A_HEADS].reshape(batch * nsteps, DELTA_ROWS, A_HEADS).transpose(0, 2, 1)
        alt = jnp.concatenate([alt[:, 0::2], alt[:, 1::2]], axis=1)
        oa = _delta(p_main, ab, alt, pad_lanes(a_log[l]), pad_lanes(dt_bias[l]),
                    even_odd_col(a_log[l]), even_odd_col(dt_bias[l]),
                    head_norm[l].reshape(1, A_DK), masks, batch, nsteps)
        ob = _band(p_main, bias, batch, nblk)
        x1 = _merge(oa, ob, p_main, xf, wa, wb, wo, tm, l)

        kv = _norm_matmul(memf, g_mem, wmkv, min(512, batch * mem_len), l)
        x2 = _xattn(x1, g_x, wmq, kv, wmo, tm, seq, mem_len, l)

        xf = _ffn(x2, g_ffn, wgu, wd, norm_final.reshape(1, d), tm,
                  final_norm=(l == depth - 1), l=l)
    return xf.reshape(batch, seq, d)
```

```python
import functools

import jax
import jax.numpy as jnp
from jax import lax
from jax.experimental import pallas as pl
from jax.experimental.pallas import tpu as pltpu

F32 = jnp.float32
BF16 = jnp.bfloat16

D_MODEL = 1024
CHUNK = 64
EPS = 1e-6
A_HEADS = 8
A_DK = 128
A_W = A_HEADS * A_DK
CONV_K = 4
B_HEADS = 16
B_DH = 64
BAND_CHUNKS = 8
BAND_LEN = (BAND_CHUNKS + 1) * CHUNK
MAX_REL = 256
N_REL = (CHUNK - 1) + MAX_REL + 1
M_HEADS = 4
M_DH = D_MODEL // M_HEADS
D_FF = 2816
AB_PAD = 128
QBLK = BAND_CHUNKS * CHUNK
BAND_WIN = BAND_LEN + CHUNK
BAND_GROUP = 2
LOG2E = 1.4426950408889634
QB_SCALE = (B_DH ** -0.5) * LOG2E

COL_QKV_A, COL_ZA, COL_QB, COL_KB, COL_VB, COL_GA, COL_GB = 0, 3, 4, 5, 6, 7, 8
N_MAIN = 9 * D_MODEL

VMEM_LIMIT = 56 * 1024 * 1024


def _params(sem):
    return pltpu.CompilerParams(dimension_semantics=sem, vmem_limit_bytes=VMEM_LIMIT)


def _const_spec(shape):
    nd = len(shape)
    return pl.BlockSpec(shape, lambda *_: (0,) * nd, pipeline_mode=pl.Buffered(1))


def _layer_spec(shape, l):
    nd = len(shape)
    return pl.BlockSpec((None,) + tuple(shape), lambda *_: (l,) + (0,) * nd,
                        pipeline_mode=pl.Buffered(1))


def _dot(a, b):
    return jnp.dot(a.astype(BF16), b.astype(BF16), preferred_element_type=F32)


def _dot_nt(a, b):
    return lax.dot_general(a.astype(BF16), b.astype(BF16), (((1,), (1,)), ((), ())),
                           preferred_element_type=F32)


def _sigmoid(x):
    return 1.0 / (1.0 + jnp.exp(-x))


def _softplus(x):
    return jnp.maximum(x, 0.0) + jnp.log1p(jnp.exp(-jnp.abs(x)))


def _rmsnorm(x, g):
    return x * lax.rsqrt(jnp.mean(x * x, axis=-1, keepdims=True) + EPS) * g


CONV_COLS = 256


def _inproj_kernel(x_ref, g_ref, wlo_ref, whi_ref, wab_ref, cw_ref, o_ref, ab_ref, tail_ref, *,
                   tiles_per_seq):
    tm = x_ref.shape[0]

    @pl.when(pl.program_id(0) % tiles_per_seq == 0)
    def _():
        tail_ref[...] = jnp.zeros_like(tail_ref)

    h = _rmsnorm(x_ref[...], g_ref[...]).astype(BF16)
    ab_ref[...] = jnp.dot(h, wab_ref[...], preferred_element_type=F32)
    for lo in range(0, 3 * A_W, CONV_COLS):
        cols = slice(lo, lo + CONV_COLS)
        o = jnp.dot(h, wlo_ref[:, cols], preferred_element_type=F32)
        xx = jnp.concatenate([tail_ref[:, cols], o], axis=0)
        tail_ref[:, cols] = o[tm - 8:, :]
        acc = o * cw_ref[CONV_K - 1:CONV_K, cols]
        for s in range(1, CONV_K):
            shifted = pltpu.roll(xx, s, axis=0)[8:, :]
            acc = acc + shifted * cw_ref[CONV_K - 1 - s:CONV_K - s, cols]
        o_ref[:, cols] = (acc * _sigmoid(acc)).astype(o_ref.dtype)
    n_lo = wlo_ref.shape[1]
    for lo in range(3 * A_W, N_MAIN, D_MODEL):
        cols = slice(lo, lo + D_MODEL)
        if lo < n_lo:
            w = wlo_ref[:, cols]
        else:
            w = whi_ref[:, lo - n_lo:lo - n_lo + D_MODEL]
        o = jnp.dot(h, w, preferred_element_type=F32)
        if lo == COL_QB * D_MODEL:
            o = o * QB_SCALE
        o_ref[:, cols] = o.astype(o_ref.dtype)


def _inproj(x, g, w_lo, w_hi, w_ab, conv_w, tm, seq, l):
    t = x.shape[0]
    n = w_lo.shape[-1] + w_hi.shape[-1]
    return pl.pallas_call(
        functools.partial(_inproj_kernel, tiles_per_seq=seq // tm),
        grid=(t // tm,),
        in_specs=[
            pl.BlockSpec((tm, D_MODEL), lambda i: (i, 0)),
            _layer_spec((1, D_MODEL), l),
            _layer_spec((D_MODEL, w_lo.shape[-1]), l),
            _layer_spec((D_MODEL, w_hi.shape[-1]), l),
            _layer_spec((D_MODEL, AB_PAD), l),
            _layer_spec((CONV_K, 3 * A_W), l),
        ],
        out_specs=[
            pl.BlockSpec((tm, n), lambda i: (i, 0)),
            pl.BlockSpec((tm, AB_PAD), lambda i: (i, 0)),
        ],
        out_shape=[
            jax.ShapeDtypeStruct((t, n), BF16),
            jax.ShapeDtypeStruct((t, AB_PAD), F32),
        ],
        scratch_shapes=[pltpu.VMEM((8, 3 * A_W), F32)],
        compiler_params=_params(("arbitrary",)),
        name="inproj",
    )(x, g, w_lo, w_hi, w_ab, conv_w)


def _norm_matmul_kernel(x_ref, g_ref, w_ref, o_ref):
    h = _rmsnorm(x_ref[...], g_ref[...]).astype(BF16)
    o_ref[...] = jnp.dot(h, w_ref[...], preferred_element_type=F32).astype(o_ref.dtype)


def _norm_matmul(x, g, w, tm, l):
    t = x.shape[0]
    n = w.shape[-1]
    return pl.pallas_call(
        _norm_matmul_kernel,
        grid=(t // tm,),
        in_specs=[
            pl.BlockSpec((tm, D_MODEL), lambda i: (i, 0)),
            _layer_spec((1, D_MODEL), l),
            _layer_spec((D_MODEL, n), l),
        ],
        out_specs=pl.BlockSpec((tm, n), lambda i: (i, 0)),
        out_shape=jax.ShapeDtypeStruct((t, n), BF16),
        compiler_params=_params(("parallel",)),
        name="mem_kv",
    )(x, g, w)


DELTA_CHUNKS = 4
DELTA_ROWS = DELTA_CHUNKS * CHUNK


def _split_hi_lo(x):
    hi = x.astype(BF16)
    lo = (x - hi.astype(F32)).astype(BF16)
    return hi, lo


def _pair_lhs(split):
    hi, lo = split
    return jnp.concatenate([hi, hi, lo], axis=1)


def _pair_rhs(split, lo_half):
    def bd(x):
        return jnp.concatenate([jnp.where(lo_half, x, 0), jnp.where(lo_half, 0, x)], axis=0)

    hi, lo = split
    return jnp.concatenate([bd(hi), bd(lo), bd(hi)], axis=0)


def _pair_mask(split, mask):
    hi, lo = split
    return jnp.where(mask, hi, 0), jnp.where(mask, lo, 0)


def _lhs_split(a, lo_half):
    hi, lo = _split_hi_lo(a)
    s = jnp.where(lo_half, hi, lo)
    return jnp.concatenate([s, s], axis=1)


def _rhs_split(b):
    hi, lo = _split_hi_lo(b)
    return jnp.concatenate([hi, hi, lo, lo], axis=0)


def _mm(a_split, b_split):
    return jnp.dot(a_split, b_split, preferred_element_type=F32)


def _split3(x):
    p1 = x.astype(BF16)
    r1 = x - p1.astype(F32)
    p2 = r1.astype(BF16)
    p3 = (r1 - p2.astype(F32)).astype(BF16)
    return p1, p2, p3


def _cumsum_masks():
    rows = DELTA_ROWS
    rr = lax.broadcasted_iota(jnp.int32, (rows, rows), 0)
    cc = lax.broadcasted_iota(jnp.int32, (rows, rows), 1)
    tril_bd = jnp.logical_and(rr >= cc, (rr >> 6) == (cc >> 6)).astype(BF16)
    rs = lax.broadcasted_iota(jnp.int32, (rows, 2 * rows), 0)
    cs = lax.broadcasted_iota(jnp.int32, (rows, 2 * rows), 1)
    upto = jnp.logical_and((rs >> 6) == (cs >> 7), (rs & 63) <= (cs & 63))
    first = (cs & CHUNK) == 0
    tri_even = jnp.logical_and(upto, first).astype(BF16)
    tri_odd = jnp.logical_and(upto, jnp.logical_not(first)).astype(BF16)
    return (jnp.concatenate([tril_bd] * 3, axis=1), jnp.concatenate([tri_even] * 3, axis=0),
            jnp.concatenate([tri_odd] * 3, axis=0))


def _delta_kernel(qkv_ref, za_ref, ab_ref, alt_ref, alog_r_ref, dt_r_ref, alog_c_ref, dt_c_ref,
                  hn_ref, tril_ref, tri_even_ref, tri_odd_ref, o_ref, s_ref):
    @pl.when(pl.program_id(1) == 0)
    def _():
        s_ref[...] = jnp.zeros_like(s_ref)

    ab = ab_ref[...]
    g_c = -jnp.exp(alog_r_ref[...]) * _softplus(ab + dt_r_ref[...])
    cum_c = jnp.dot(tril_ref[...], jnp.concatenate(_split3(g_c), axis=0),
                    preferred_element_type=F32)
    g_r = -jnp.exp(alog_c_ref[...]) * _softplus(alt_ref[0] + dt_c_ref[...])
    g_r3 = jnp.concatenate(_split3(g_r), axis=1)
    npair = A_HEADS // 2
    cum_rp = (jnp.dot(g_r3, tri_even_ref[...], preferred_element_type=F32)[:npair]
              + jnp.dot(g_r3, tri_odd_ref[...], preferred_element_type=F32)[npair:])
    beta_c = _sigmoid(ab)

    row = lax.broadcasted_iota(jnp.int32, (CHUNK, 2 * CHUNK), 0)
    lane = lax.broadcasted_iota(jnp.int32, (CHUNK, 2 * CHUNK), 1)
    col = lane & (CHUNK - 1)
    lo_half = lane < CHUNK
    incl = row >= col
    strict = row > col
    eye = (row == col).astype(F32)
    same16 = (row >> 4) == (col >> 4)
    same32 = (row >> 5) == (col >> 5)
    off16 = jnp.logical_and(same32, jnp.logical_not(same16))

    eye_bf = eye.astype(BF16)
    zero_blk = jnp.zeros((CHUNK, A_DK), F32)

    qn, kn, vv = [], [], []
    for h in range(A_HEADS):
        q = qkv_ref[:, h * A_DK:(h + 1) * A_DK].astype(F32)
        k = qkv_ref[:, A_W + h * A_DK:A_W + (h + 1) * A_DK].astype(F32)
        qn.append(q * lax.rsqrt(jnp.sum(q * q, axis=-1, keepdims=True) + EPS) * (A_DK ** -0.5))
        kn.append(k * lax.rsqrt(jnp.sum(k * k, axis=-1, keepdims=True) + EPS))
        vv.append(qkv_ref[:, 2 * A_W + h * A_DK:2 * A_W + (h + 1) * A_DK].astype(F32))

    cum_b, beta_b = {}, {}
    for c in range(DELTA_CHUNKS):
        rsl = slice(c * CHUNK, (c + 1) * CHUNK)
        for h in range(A_HEADS):
            cum_b[c, h] = jnp.broadcast_to(cum_c[rsl, h:h + 1], (CHUNK, A_DK))
            beta_b[c, h] = jnp.broadcast_to(beta_c[rsl, A_HEADS + h:A_HEADS + h + 1],
                                            (CHUNK, A_DK))

    lmat, pmat = [], []
    for c in range(DELTA_CHUNKS):
        rsl = slice(c * CHUNK, (c + 1) * CHUNK)
        for i in range(npair):
            a, b = 2 * i, 2 * i + 1
            ka, kb = kn[a][rsl], kn[b][rsl]
            lhs = jnp.concatenate([jnp.concatenate([ka, kb], axis=1),
                                   jnp.concatenate([qn[a][rsl], qn[b][rsl]], axis=1)], axis=0)
            rhs_nt = jnp.concatenate([jnp.concatenate([ka, zero_blk], axis=1),
                                      jnp.concatenate([zero_blk, kb], axis=1)], axis=0)
            prod = lax.dot_general(lhs.astype(BF16), rhs_nt.astype(BF16),
                                   (((1,), (1,)), ((), ())), preferred_element_type=F32)
            cum2 = jnp.where(lo_half, cum_b[c, a], cum_b[c, b])
            diff = cum2 - cum_rp[i:i + 1, c * 2 * CHUNK:(c + 1) * 2 * CHUNK]
            decay = jnp.where(incl, jnp.exp(jnp.where(incl, diff, 0.0)), 0.0)
            beta2 = jnp.where(lo_half, beta_b[c, a], beta_b[c, b])
            lmat.append(jnp.where(strict, beta2 * prod[:CHUNK] * decay, 0.0))
            pm = prod[CHUNK:] * decay
            pmat.append((jnp.where(lo_half, pm, 0.0).astype(BF16),
                         jnp.where(lo_half, 0.0, pm).astype(BF16)))

    rhs, kdec, qdec, elast = [], [], [], []
    for c in range(DELTA_CHUNKS):
        rsl = slice(c * CHUNK, (c + 1) * CHUNK)
        for h in range(A_HEADS):
            kc = kn[h][rsl]
            cum = cum_b[c, h]
            beta = beta_b[c, h]
            eg = jnp.exp(cum)
            g_last = cum[CHUNK - 1:CHUNK, :]
            rhs.append(_rhs_split(jnp.concatenate([beta * vv[h][rsl], (beta * eg) * kc], axis=1)))
            qdec.append(qn[h][rsl] * eg)
            kdec.append((kc * jnp.exp(g_last - cum)).astype(BF16))
            elast.append(jnp.exp(g_last))

    def mul(xs, ys):
        return [_mm(_pair_lhs(x), _pair_rhs(y, lo_half)) for x, y in zip(xs, ys)]

    def mul2(xs, zs, ys):
        both = [_mm(jnp.concatenate([_pair_lhs(x), _pair_lhs(z)], axis=0), _pair_rhs(y, lo_half))
                for x, z, y in zip(xs, zs, ys)]
        return [b[:CHUNK] for b in both], [b[CHUNK:] for b in both]

    def split(ms):
        return [_split_hi_lo(m) for m in ms]

    l_s = split(lmat)
    d1 = [_pair_mask(s, same16) for s in l_s]
    d2 = split(mul(d1, d1))
    r0 = [(eye_bf - hi, -lo) for hi, lo in d1]
    r0f = [eye - jnp.where(same16, m, 0.0) for m in lmat]
    d4f, r0d2 = mul2(d2, r0, d2)
    d4 = split(d4f)
    r1f = [r + m for r, m in zip(r0f, r0d2)]
    r1 = split(r1f)
    d8f, r1d4 = mul2(d4, r1, d4)
    d8 = split(d8f)
    r2f = [r + m for r, m in zip(r1f, r1d4)]
    r2 = split(r2f)
    r3f = [r + m for r, m in zip(r2f, mul(r2, d8))]
    r3 = split(r3f)
    t1 = split(mul([_pair_mask(s, off16) for s in l_s], r3))
    r4f = [r - m for r, m in zip(r3f, mul(r3, t1))]
    r4 = split(r4f)
    t2 = split(mul([_pair_mask(s, jnp.logical_not(same32)) for s in l_s], r4))
    r5f = [r - m for r, m in zip(r4f, mul(r4, t2))]

    sol = []
    for idx, t_pair in enumerate(r5f):
        c, i = divmod(idx, npair)
        swapped = pltpu.roll(t_pair, CHUNK, axis=1)
        for h, t_dup in ((2 * i, jnp.where(lo_half, t_pair, swapped)),
                         (2 * i + 1, jnp.where(lo_half, swapped, t_pair))):
            sol.append((c * A_HEADS + h, _mm(_lhs_split(t_dup, lo_half), rhs[c * A_HEADS + h])))
    sol = [s for _, s in sorted(sol, key=lambda e: e[0])]

    lin, s_add, o_add = [], [], []
    for idx, wu in enumerate(sol):
        c, h = divmod(idx, A_HEADS)
        wu_bf = wu.astype(BF16)
        g = lax.dot_general(kdec[idx], wu_bf, (((0,), (0,)), ((), ())),
                            preferred_element_type=F32)
        pm = pmat[c * npair + h // 2][h % 2]
        pw = jnp.dot(pm, jnp.concatenate([wu_bf, wu_bf], axis=0),
                     preferred_element_type=F32)
        lin.append(jnp.concatenate([g[:, A_DK:], qdec[idx] - pw[:, A_DK:]], axis=0).astype(BF16))
        s_add.append(g[:, :A_DK])
        o_add.append(pw[:, :A_DK])

    state = [s_ref[h] for h in range(A_HEADS)]
    for c in range(DELTA_CHUNKS):
        rsl = slice(c * CHUNK, (c + 1) * CHUNK)
        for h in range(A_HEADS):
            i = c * A_HEADS + h
            lanes = slice(h * A_DK, (h + 1) * A_DK)
            y = jnp.dot(lin[i], state[h].astype(BF16), preferred_element_type=F32)
            state[h] = state[h] * elast[i] - y[:A_DK] + s_add[i]
            o = y[A_DK:] + o_add[i]
            on = _rmsnorm(o, hn_ref[...])
            z = za_ref[rsl, lanes].astype(F32)
            o_ref[rsl, lanes] = (on * (z * _sigmoid(z))).astype(o_ref.dtype)
    for h in range(A_HEADS):
        s_ref[h] = state[h]


def _delta(p_main, ab, alt, alog_r, dt_r, alog_c, dt_c, head_norm, masks, batch, nc):
    t = p_main.shape[0]
    rows = DELTA_ROWS
    tril3, tri_even3, tri_odd3 = masks
    return pl.pallas_call(
        _delta_kernel,
        grid=(batch, nc),
        in_specs=[
            pl.BlockSpec((rows, 3 * A_W), lambda b, n: (b * nc + n, COL_QKV_A)),
            pl.BlockSpec((rows, A_W), lambda b, n: (b * nc + n, COL_ZA)),
            pl.BlockSpec((rows, AB_PAD), lambda b, n: (b * nc + n, 0)),
            pl.BlockSpec((1, A_HEADS, rows), lambda b, n: (b * nc + n, 0, 0)),
            _const_spec((1, AB_PAD)),
            _const_spec((1, AB_PAD)),
            _const_spec((A_HEADS, 1)),
            _const_spec((A_HEADS, 1)),
            _const_spec((1, A_DK)),
            _const_spec(tril3.shape),
            _const_spec(tri_even3.shape),
            _const_spec(tri_odd3.shape),
        ],
        out_specs=pl.BlockSpec((rows, A_W), lambda b, n: (b * nc + n, 0)),
        out_shape=jax.ShapeDtypeStruct((t, A_W), BF16),
        scratch_shapes=[pltpu.VMEM((A_HEADS, A_DK, A_DK), F32)],
        compiler_params=_params(("parallel", "arbitrary")),
        name="delta",
    )(p_main, p_main, ab, alt, alog_r, dt_r, alog_c, dt_c, head_norm, tril3, tri_even3, tri_odd3)


def _band_bias_kernel(win_ref, o_ref):
    rows = o_ref.shape[0] // CHUNK
    shape = (rows, CHUNK, 2 * CHUNK)
    i = lax.broadcasted_iota(jnp.int32, shape, 0) + pl.program_id(0) * rows
    j = lax.broadcasted_iota(jnp.int32, shape, 1)
    cand = lax.broadcasted_iota(jnp.int32, shape, 2)
    onehot = (i - j + (CHUNK - 1) == cand).astype(BF16).reshape(rows * CHUNK, 2 * CHUNK)
    acc = None
    for piece in _split3(win_ref[...]):
        part = jnp.dot(onehot, piece, preferred_element_type=F32)
        acc = part if acc is None else acc + part
    o_ref[...] = acc * LOG2E


def _band_bias(rel_bias):
    nslot = BAND_CHUNKS + 1
    relt = jnp.pad(rel_bias.T, ((0, nslot * CHUNK + CHUNK - N_REL), (0, 0)), mode="edge")
    win = jnp.concatenate([relt[(BAND_CHUNKS - w) * CHUNK:(BAND_CHUNKS - w + 2) * CHUNK]
                           for w in range(nslot)], axis=1)
    ncol = 2 * 2 * CHUNK
    win = jnp.pad(win, ((0, 0), (0, ncol - nslot * B_HEADS)))
    rows = 16
    out = pl.pallas_call(
        _band_bias_kernel,
        grid=(CHUNK // rows,),
        in_specs=[_const_spec((2 * CHUNK, ncol))],
        out_specs=pl.BlockSpec((rows * CHUNK, ncol), lambda g: (g, 0)),
        out_shape=jax.ShapeDtypeStruct((CHUNK * CHUNK, ncol), F32),
        compiler_params=_params(("parallel",)),
        name="band_bias",
    )(win)
    out = out[:, :nslot * B_HEADS].reshape(CHUNK, CHUNK, nslot, B_HEADS)
    return out.transpose(3, 0, 2, 1).reshape(B_HEADS, CHUNK, BAND_LEN)


def _band_kernel(q_ref, kp_ref, kc_ref, vp_ref, vc_ref, bias_ref, o_ref, kwin, vwin):
    blk = pl.program_id(1)
    kwin[:QBLK, :] = kp_ref[...]
    kwin[QBLK:, :] = kc_ref[...]
    vwin[:QBLK, :] = vp_ref[...]
    vwin[QBLK:, :] = vc_ref[...]
    col_chunk = lax.broadcasted_iota(jnp.int32, (4 * CHUNK, BAND_WIN), 1) >> 6
    lane = lax.broadcasted_iota(jnp.int32, (CHUNK, 2 * B_DH), 1)
    even = lane < B_DH
    pairs = B_HEADS // 2

    def body(a2, carry, masked):
        r0 = pl.multiple_of(a2 * 2 * CHUNK, 2 * CHUNK)
        for g0 in range(0, pairs, BAND_GROUP):
            sc = []
            for pr in range(g0, g0 + BAND_GROUP):
                lanes = slice(pr * 2 * B_DH, (pr + 1) * 2 * B_DH)
                qa = q_ref[pl.ds(r0, 2 * CHUNK), lanes]
                q4 = jnp.concatenate(
                    [jnp.where(even, qa[:CHUNK], 0), jnp.where(even, 0, qa[:CHUNK]),
                     jnp.where(even, qa[CHUNK:], 0), jnp.where(even, 0, qa[CHUNK:])], axis=0)
                kp = kwin[pl.ds(r0, BAND_WIN), lanes]
                s = lax.dot_general(q4, kp, (((1,), (1,)), ((), ())),
                                    preferred_element_type=F32) + bias_ref[pr]
                if masked:
                    s = jnp.where(col_chunk + 2 * a2 >= BAND_CHUNKS, s, -1e30)
                sc.append(s)
            m = [jnp.max(s, axis=-1, keepdims=True) for s in sc]
            p = [jnp.exp2(s - mm) for s, mm in zip(sc, m)]
            denom = [jnp.sum(pp, axis=-1, keepdims=True) for pp in p]
            for i, pr in enumerate(range(g0, g0 + BAND_GROUP)):
                lanes = slice(pr * 2 * B_DH, (pr + 1) * 2 * B_DH)
                vp = vwin[pl.ds(r0, BAND_WIN), lanes]
                o4 = jnp.dot(p[i].astype(BF16), vp, preferred_element_type=F32) / denom[i]
                o2 = jnp.concatenate(
                    [jnp.where(even, o4[:CHUNK], o4[CHUNK:2 * CHUNK]),
                     jnp.where(even, o4[2 * CHUNK:3 * CHUNK], o4[3 * CHUNK:])], axis=0)
                o_ref[pl.ds(r0, 2 * CHUNK), lanes] = o2.astype(o_ref.dtype)
        return carry

    steps = QBLK // (2 * CHUNK)

    @pl.when(blk == 0)
    def _():
        lax.fori_loop(0, steps, functools.partial(body, masked=True), 0)

    @pl.when(blk > 0)
    def _():
        lax.fori_loop(0, steps, functools.partial(body, masked=False), 0)


def _band_tables(bias):
    neg = jnp.full((B_HEADS, CHUNK, CHUNK), -1e30, F32)
    first = jnp.concatenate([bias, neg], axis=2)
    second = jnp.concatenate([neg, bias], axis=2)
    t = jnp.stack([first[0::2], first[1::2], second[0::2], second[1::2]], axis=1)
    return t.reshape(B_HEADS // 2, 4 * CHUNK, BAND_WIN)


def _band(p_main, bias, batch, nblk):
    t = p_main.shape[0]

    def prev(b, i):
        return b * nblk + jnp.maximum(i - 1, 0)

    return pl.pallas_call(
        _band_kernel,
        grid=(batch, nblk),
        in_specs=[
            pl.BlockSpec((QBLK, D_MODEL), lambda b, i: (b * nblk + i, COL_QB)),
            pl.BlockSpec((QBLK, D_MODEL), lambda b, i: (prev(b, i), COL_KB)),
            pl.BlockSpec((QBLK, D_MODEL), lambda b, i: (b * nblk + i, COL_KB)),
            pl.BlockSpec((QBLK, D_MODEL), lambda b, i: (prev(b, i), COL_VB)),
            pl.BlockSpec((QBLK, D_MODEL), lambda b, i: (b * nblk + i, COL_VB)),
            _const_spec((B_HEADS // 2, 4 * CHUNK, BAND_WIN)),
        ],
        out_specs=pl.BlockSpec((QBLK, D_MODEL), lambda b, i: (b * nblk + i, 0)),
        out_shape=jax.ShapeDtypeStruct((t, D_MODEL), BF16),
        scratch_shapes=[
            pltpu.VMEM((2 * QBLK, D_MODEL), BF16),
            pltpu.VMEM((2 * QBLK, D_MODEL), BF16),
        ],
        compiler_params=_params(("parallel", "parallel")),
        name="band",
    )(p_main, p_main, p_main, p_main, p_main, bias)


def _merge_kernel(oa_ref, ob_ref, ga_ref, gb_ref, x_ref, wa_ref, wb_ref, wo_ref, o_ref):
    ya = jnp.dot(oa_ref[...], wa_ref[...], preferred_element_type=F32)
    yb = jnp.dot(ob_ref[...], wb_ref[...], preferred_element_type=F32)
    y = _sigmoid(ga_ref[...].astype(F32)) * ya + _sigmoid(gb_ref[...].astype(F32)) * yb
    o_ref[...] = x_ref[...] + jnp.dot(y.astype(BF16), wo_ref[...], preferred_element_type=F32)


def _merge(oa, ob, p_main, x, wa, wb, wo, tm, l):
    t = x.shape[0]
    row = lambda i: (i, 0)
    return pl.pallas_call(
        _merge_kernel,
        grid=(t // tm,),
        in_specs=[
            pl.BlockSpec((tm, D_MODEL), row),
            pl.BlockSpec((tm, D_MODEL), row),
            pl.BlockSpec((tm, D_MODEL), lambda i: (i, COL_GA)),
            pl.BlockSpec((tm, D_MODEL), lambda i: (i, COL_GB)),
            pl.BlockSpec((tm, D_MODEL), row),
            _layer_spec((D_MODEL, D_MODEL), l),
            _layer_spec((D_MODEL, D_MODEL), l),
            _layer_spec((D_MODEL, D_MODEL), l),
        ],
        out_specs=pl.BlockSpec((tm, D_MODEL), row),
        out_shape=jax.ShapeDtypeStruct((t, D_MODEL), F32),
        compiler_params=_params(("parallel",)),
        name="merge",
    )(oa, ob, p_main, p_main, x, wa, wb, wo)


def _xattn_kernel(x_ref, g_ref, wq_ref, kv_ref, wo_ref, o_ref):
    x = x_ref[...]
    h = _rmsnorm(x, g_ref[...]).astype(BF16)
    q = jnp.dot(h, wq_ref[...], preferred_element_type=F32).astype(BF16)
    scale = M_DH ** -0.5
    outs = []
    for hh in range(M_HEADS):
        kh = kv_ref[:, hh * M_DH:(hh + 1) * M_DH]
        vh = kv_ref[:, D_MODEL + hh * M_DH:D_MODEL + (hh + 1) * M_DH]
        sc = _dot_nt(q[:, hh * M_DH:(hh + 1) * M_DH], kh) * scale
        m = jnp.max(sc, axis=-1, keepdims=True)
        p = jnp.exp(sc - m)
        denom = jnp.sum(p, axis=-1, keepdims=True)
        outs.append((_dot(p, vh) / denom).astype(BF16))
    o = jnp.concatenate(outs, axis=1)
    o_ref[...] = x + jnp.dot(o, wo_ref[...], preferred_element_type=F32)


def _xattn(x, g, wq, kv, wo, tm, seq, mem_len, l):
    t = x.shape[0]
    per_batch = seq // tm
    return pl.pallas_call(
        _xattn_kernel,
        grid=(t // tm,),
        in_specs=[
            pl.BlockSpec((tm, D_MODEL), lambda i: (i, 0)),
            _layer_spec((1, D_MODEL), l),
            _layer_spec((D_MODEL, D_MODEL), l),
            pl.BlockSpec((mem_len, 2 * D_MODEL), lambda i: (i // per_batch, 0)),
            _layer_spec((D_MODEL, D_MODEL), l),
        ],
        out_specs=pl.BlockSpec((tm, D_MODEL), lambda i: (i, 0)),
        out_shape=jax.ShapeDtypeStruct((t, D_MODEL), F32),
        compiler_params=_params(("parallel",)),
        name="xattn",
    )(x, g, wq, kv, wo)


FF_CHUNK = 256


def _ffn_kernel(x_ref, g_ref, wgu_ref, wd_ref, gf_ref, o_ref, act_ref, *, final_norm):
    x = x_ref[...]
    h = _rmsnorm(x, g_ref[...]).astype(BF16)
    for c in range(D_FF // FF_CHUNK):
        lo = c * FF_CHUNK
        gate = jnp.dot(h, wgu_ref[:, lo:lo + FF_CHUNK], preferred_element_type=F32)
        up = jnp.dot(h, wgu_ref[:, D_FF + lo:D_FF + lo + FF_CHUNK], preferred_element_type=F32)
        act_ref[:, lo:lo + FF_CHUNK] = (gate * _sigmoid(gate) * up).astype(BF16)
    y = x + jnp.dot(act_ref[...], wd_ref[...], preferred_element_type=F32)
    if final_norm:
        y = _rmsnorm(y, gf_ref[...])
    o_ref[...] = y


def _ffn(x, g, wgu, wd, gf, tm, final_norm, l):
    t = x.shape[0]
    return pl.pallas_call(
        functools.partial(_ffn_kernel, final_norm=final_norm),
        grid=(t // tm,),
        in_specs=[
            pl.BlockSpec((tm, D_MODEL), lambda i: (i, 0)),
            _layer_spec((1, D_MODEL), l),
            _layer_spec((D_MODEL, 2 * D_FF), l),
            _layer_spec((D_FF, D_MODEL), l),
            _const_spec((1, D_MODEL)),
        ],
        out_specs=pl.BlockSpec((tm, D_MODEL), lambda i: (i, 0)),
        out_shape=jax.ShapeDtypeStruct((t, D_MODEL), F32),
        scratch_shapes=[pltpu.VMEM((tm, D_FF), BF16)],
        compiler_params=_params(("parallel",)),
        name="ffn",
    )(x, g, wgu, wd, gf)


def kernel(x, mem, norm_mix, w_in, conv_w, a_log, dt_bias, head_norm, w_a_out, w_b_out,
           rel_bias, w_o, norm_xattn, norm_mem, w_mq, w_mkv, w_mo, norm_ffn, w_gate_up,
           w_down, norm_final):
    batch, seq, d = x.shape
    mem_len = mem.shape[1]
    depth = w_in.shape[0]
    t = batch * seq
    nsteps = seq // DELTA_ROWS
    nblk = seq // QBLK
    tm = min(512, seq)

    xf = x.reshape(t, d)
    memf = mem.reshape(batch * mem_len, d)
    bias = _band_tables(_band_bias(rel_bias))
    masks = _cumsum_masks()

    ab_lo = 4 * A_W
    ab_hi = ab_lo + 2 * A_HEADS
    w_lo, w_hi = w_in[:, :, :ab_lo].astype(BF16), w_in[:, :, ab_hi:].astype(BF16)
    w_ab = jnp.pad(w_in[:, :, ab_lo:ab_hi],
                   ((0, 0), (0, 0), (0, AB_PAD - 2 * A_HEADS))).astype(BF16)
    wa, wb, wo = w_a_out.astype(BF16), w_b_out.astype(BF16), w_o.astype(BF16)
    wmq, wmkv, wmo = w_mq.astype(BF16), w_mkv.astype(BF16), w_mo.astype(BF16)
    wgu, wd = w_gate_up.astype(BF16), w_down.astype(BF16)

    def rows3(v):
        return v.reshape(depth, 1, -1)

    g_mix, g_x, g_mem, g_ffn = rows3(norm_mix), rows3(norm_xattn), rows3(norm_mem), rows3(norm_ffn)

    def pad_lanes(v):
        return jnp.pad(v.reshape(1, -1), ((0, 0), (0, AB_PAD - v.shape[-1])))

    def even_odd_col(v):
        return jnp.concatenate([v[0::2], v[1::2]]).reshape(A_HEADS, 1)

    for l in range(depth):
        p_main, ab = _inproj(xf, g_mix, w_lo, w_hi, w_ab, conv_w, tm, seq, l)

        alt = ab[:, :A_HEADS].reshape(batch * nsteps, DELTA_ROWS, A_HEADS).transpose(0, 2, 1)
        alt = jnp.concatenate([alt[:, 0::2], alt[:, 1::2]], axis=1)
        oa = _delta(p_main, ab, alt, pad_lanes(a_log[l]), pad_lanes(dt_bias[l]),
                    even_odd_col(a_log[l]), even_odd_col(dt_bias[l]),
                    head_norm[l].reshape(1, A_DK), masks, batch, nsteps)
        ob = _band(p_main, bias, batch, nblk)
        x1 = _merge(oa, ob, p_main, xf, wa, wb, wo, tm, l)

        kv = _norm_matmul(memf, g_mem, wmkv, min(512, batch * mem_len), l)
        x2 = _xattn(x1, g_x, wmq, kv, wmo, tm, seq, mem_len, l)

        xf = _ffn(x2, g_ffn, wgu, wd, norm_final.reshape(1, d), tm,
                  final_norm=(l == depth - 1), l=l)
    return xf.reshape(batch, seq, d)
```

```python
import functools

import jax
import jax.numpy as jnp
from jax import lax
from jax.experimental import pallas as pl
from jax.experimental.pallas import tpu as pltpu

F32 = jnp.float32
BF16 = jnp.bfloat16

D_MODEL = 1024
CHUNK = 64
EPS = 1e-6
A_HEADS = 8
A_DK = 128
A_W = A_HEADS * A_DK
CONV_K = 4
B_HEADS = 16
B_DH = 64
BAND_CHUNKS = 8
BAND_LEN = (BAND_CHUNKS + 1) * CHUNK
MAX_REL = 256
N_REL = (CHUNK - 1) + MAX_REL + 1
M_HEADS = 4
M_DH = D_MODEL // M_HEADS
D_FF = 2816
AB_PAD = 128
QBLK = BAND_CHUNKS * CHUNK
BAND_WIN = BAND_LEN + CHUNK
BAND_GROUP = 2
LOG2E = 1.4426950408889634
QB_SCALE = (B_DH ** -0.5) * LOG2E

COL_QKV_A, COL_ZA, COL_QB, COL_KB, COL_VB, COL_GA, COL_GB = 0, 3, 4, 5, 6, 7, 8
N_MAIN = 9 * D_MODEL

VMEM_LIMIT = 56 * 1024 * 1024


def _params(sem):
    return pltpu.CompilerParams(dimension_semantics=sem, vmem_limit_bytes=VMEM_LIMIT)


def _const_spec(shape):
    nd = len(shape)
    return pl.BlockSpec(shape, lambda *_: (0,) * nd, pipeline_mode=pl.Buffered(1))


def _layer_spec(shape, l):
    nd = len(shape)
    return pl.BlockSpec((None,) + tuple(shape), lambda *_: (l,) + (0,) * nd,
                        pipeline_mode=pl.Buffered(1))


def _dot(a, b):
    return jnp.dot(a.astype(BF16), b.astype(BF16), preferred_element_type=F32)


def _dot_nt(a, b):
    return lax.dot_general(a.astype(BF16), b.astype(BF16), (((1,), (1,)), ((), ())),
                           preferred_element_type=F32)


def _sigmoid(x):
    return 1.0 / (1.0 + jnp.exp(-x))


def _softplus(x):
    return jnp.maximum(x, 0.0) + jnp.log1p(jnp.exp(-jnp.abs(x)))


def _rmsnorm(x, g):
    return x * lax.rsqrt(jnp.mean(x * x, axis=-1, keepdims=True) + EPS) * g


CONV_COLS = 256


def _inproj_kernel(x_ref, g_ref, wlo_ref, whi_ref, wab_ref, cw_ref, o_ref, ab_ref, tail_ref, *,
                   tiles_per_seq):
    tm = x_ref.shape[0]

    @pl.when(pl.program_id(0) % tiles_per_seq == 0)
    def _():
        tail_ref[...] = jnp.zeros_like(tail_ref)

    h = _rmsnorm(x_ref[...], g_ref[...]).astype(BF16)
    ab_ref[...] = jnp.dot(h, wab_ref[...], preferred_element_type=F32)
    for lo in range(0, 3 * A_W, CONV_COLS):
        cols = slice(lo, lo + CONV_COLS)
        o = jnp.dot(h, wlo_ref[:, cols], preferred_element_type=F32)
        xx = jnp.concatenate([tail_ref[:, cols], o], axis=0)
        tail_ref[:, cols] = o[tm - 8:, :]
        acc = o * cw_ref[CONV_K - 1:CONV_K, cols]
        for s in range(1, CONV_K):
            shifted = pltpu.roll(xx, s, axis=0)[8:, :]
            acc = acc + shifted * cw_ref[CONV_K - 1 - s:CONV_K - s, cols]
        o_ref[:, cols] = (acc * _sigmoid(acc)).astype(o_ref.dtype)
    n_lo = wlo_ref.shape[1]
    for lo in range(3 * A_W, N_MAIN, D_MODEL):
        cols = slice(lo, lo + D_MODEL)
        if lo < n_lo:
            w = wlo_ref[:, cols]
        else:
            w = whi_ref[:, lo - n_lo:lo - n_lo + D_MODEL]
        o = jnp.dot(h, w, preferred_element_type=F32)
        if lo == COL_QB * D_MODEL:
            o = o * QB_SCALE
        o_ref[:, cols] = o.astype(o_ref.dtype)


def _inproj(x, g, w_lo, w_hi, w_ab, conv_w, tm, seq, l):
    t = x.shape[0]
    n = w_lo.shape[-1] + w_hi.shape[-1]
    return pl.pallas_call(
        functools.partial(_inproj_kernel, tiles_per_seq=seq // tm),
        grid=(t // tm,),
        in_specs=[
            pl.BlockSpec((tm, D_MODEL), lambda i: (i, 0)),
            _layer_spec((1, D_MODEL), l),
            _layer_spec((D_MODEL, w_lo.shape[-1]), l),
            _layer_spec((D_MODEL, w_hi.shape[-1]), l),
            _layer_spec((D_MODEL, AB_PAD), l),
            _layer_spec((CONV_K, 3 * A_W), l),
        ],
        out_specs=[
            pl.BlockSpec((tm, n), lambda i: (i, 0)),
            pl.BlockSpec((tm, AB_PAD), lambda i: (i, 0)),
        ],
        out_shape=[
            jax.ShapeDtypeStruct((t, n), BF16),
            jax.ShapeDtypeStruct((t, AB_PAD), F32),
        ],
        scratch_shapes=[pltpu.VMEM((8, 3 * A_W), F32)],
        compiler_params=_params(("arbitrary",)),
        name="inproj",
    )(x, g, w_lo, w_hi, w_ab, conv_w)


def _norm_matmul_kernel(x_ref, g_ref, w_ref, o_ref):
    h = _rmsnorm(x_ref[...], g_ref[...]).astype(BF16)
    o_ref[...] = jnp.dot(h, w_ref[...], preferred_element_type=F32).astype(o_ref.dtype)


def _norm_matmul(x, g, w, tm, l):
    t = x.shape[0]
    n = w.shape[-1]
    return pl.pallas_call(
        _norm_matmul_kernel,
        grid=(t // tm,),
        in_specs=[
            pl.BlockSpec((tm, D_MODEL), lambda i: (i, 0)),
            _layer_spec((1, D_MODEL), l),
            _layer_spec((D_MODEL, n), l),
        ],
        out_specs=pl.BlockSpec((tm, n), lambda i: (i, 0)),
        out_shape=jax.ShapeDtypeStruct((t, n), BF16),
        compiler_params=_params(("parallel",)),
        name="mem_kv",
    )(x, g, w)


DELTA_CHUNKS = 4
DELTA_ROWS = DELTA_CHUNKS * CHUNK


def _split_hi_lo(x):
    hi = x.astype(BF16)
    lo = (x - hi.astype(F32)).astype(BF16)
    return hi, lo


def _pair_lhs(split):
    hi, lo = split
    return jnp.concatenate([hi, hi, lo], axis=1)


def _pair_rhs(split, lo_half):
    def bd(x):
        return jnp.concatenate([jnp.where(lo_half, x, 0), jnp.where(lo_half, 0, x)], axis=0)

    hi, lo = split
    return jnp.concatenate([bd(hi), bd(lo), bd(hi)], axis=0)


def _pair_mask(split, mask):
    hi, lo = split
    return jnp.where(mask, hi, 0), jnp.where(mask, lo, 0)


def _lhs_split(a, lo_half):
    hi, lo = _split_hi_lo(a)
    s = jnp.where(lo_half, hi, lo)
    return jnp.concatenate([s, s], axis=1)


def _rhs_split(b):
    hi, lo = _split_hi_lo(b)
    return jnp.concatenate([hi, hi, lo, lo], axis=0)


def _mm(a_split, b_split):
    return jnp.dot(a_split, b_split, preferred_element_type=F32)


def _split3(x):
    p1 = x.astype(BF16)
    r1 = x - p1.astype(F32)
    p2 = r1.astype(BF16)
    p3 = (r1 - p2.astype(F32)).astype(BF16)
    return p1, p2, p3


def _cumsum_masks():
    rows = DELTA_ROWS
    rr = lax.broadcasted_iota(jnp.int32, (rows, rows), 0)
    cc = lax.broadcasted_iota(jnp.int32, (rows, rows), 1)
    tril_bd = jnp.logical_and(rr >= cc, (rr >> 6) == (cc >> 6)).astype(BF16)
    rs = lax.broadcasted_iota(jnp.int32, (rows, 2 * rows), 0)
    cs = lax.broadcasted_iota(jnp.int32, (rows, 2 * rows), 1)
    upto = jnp.logical_and((rs >> 6) == (cs >> 7), (rs & 63) <= (cs & 63))
    first = (cs & CHUNK) == 0
    tri_even = jnp.logical_and(upto, first).astype(BF16)
    tri_odd = jnp.logical_and(upto, jnp.logical_not(first)).astype(BF16)
    return (jnp.concatenate([tril_bd] * 3, axis=1), jnp.concatenate([tri_even] * 3, axis=0),
            jnp.concatenate([tri_odd] * 3, axis=0))


def _delta_kernel(qkv_ref, za_ref, ab_ref, alt_ref, alog_r_ref, dt_r_ref, alog_c_ref, dt_c_ref,
                  hn_ref, tril_ref, tri_even_ref, tri_odd_ref, o_ref, s_ref):
    @pl.when(pl.program_id(1) == 0)
    def _():
        s_ref[...] = jnp.zeros_like(s_ref)

    ab = ab_ref[...]
    g_c = -jnp.exp(alog_r_ref[...]) * _softplus(ab + dt_r_ref[...])
    cum_c = jnp.dot(tril_ref[...], jnp.concatenate(_split3(g_c), axis=0),
                    preferred_element_type=F32)
    g_r = -jnp.exp(alog_c_ref[...]) * _softplus(alt_ref[0] + dt_c_ref[...])
    g_r3 = jnp.concatenate(_split3(g_r), axis=1)
    npair = A_HEADS // 2
    cum_rp = (jnp.dot(g_r3, tri_even_ref[...], preferred_element_type=F32)[:npair]
              + jnp.dot(g_r3, tri_odd_ref[...], preferred_element_type=F32)[npair:])
    beta_c = _sigmoid(ab)

    row = lax.broadcasted_iota(jnp.int32, (CHUNK, 2 * CHUNK), 0)
    lane = lax.broadcasted_iota(jnp.int32, (CHUNK, 2 * CHUNK), 1)
    col = lane & (CHUNK - 1)
    lo_half = lane < CHUNK
    incl = row >= col
    strict = row > col
    eye = (row == col).astype(F32)
    same16 = (row >> 4) == (col >> 4)
    same32 = (row >> 5) == (col >> 5)
    off16 = jnp.logical_and(same32, jnp.logical_not(same16))

    eye_bf = eye.astype(BF16)
    zero_blk = jnp.zeros((CHUNK, A_DK), F32)

    qn, kn, vv = [], [], []
    for h in range(A_HEADS):
        q = qkv_ref[:, h * A_DK:(h + 1) * A_DK].astype(F32)
        k = qkv_ref[:, A_W + h * A_DK:A_W + (h + 1) * A_DK].astype(F32)
        qn.append(q * lax.rsqrt(jnp.sum(q * q, axis=-1, keepdims=True) + EPS) * (A_DK ** -0.5))
        kn.append(k * lax.rsqrt(jnp.sum(k * k, axis=-1, keepdims=True) + EPS))
        vv.append(qkv_ref[:, 2 * A_W + h * A_DK:2 * A_W + (h + 1) * A_DK].astype(F32))

    cum_b, beta_b = {}, {}
    for c in range(DELTA_CHUNKS):
        rsl = slice(c * CHUNK, (c + 1) * CHUNK)
        for h in range(A_HEADS):
            cum_b[c, h] = jnp.broadcast_to(cum_c[rsl, h:h + 1], (CHUNK, A_DK))
            beta_b[c, h] = jnp.broadcast_to(beta_c[rsl, A_HEADS + h:A_HEADS + h + 1],
                                            (CHUNK, A_DK))

    lmat, pmat = [], []
    for c in range(DELTA_CHUNKS):
        rsl = slice(c * CHUNK, (c + 1) * CHUNK)
        for i in range(npair):
            a, b = 2 * i, 2 * i + 1
            ka, kb = kn[a][rsl], kn[b][rsl]
            lhs = jnp.concatenate([jnp.concatenate([ka, kb], axis=1),
                                   jnp.concatenate([qn[a][rsl], qn[b][rsl]], axis=1)], axis=0)
            rhs_nt = jnp.concatenate([jnp.concatenate([ka, zero_blk], axis=1),
                                      jnp.concatenate([zero_blk, kb], axis=1)], axis=0)
            prod = lax.dot_general(lhs.astype(BF16), rhs_nt.astype(BF16),
                                   (((1,), (1,)), ((), ())), preferred_element_type=F32)
            cum2 = jnp.where(lo_half, cum_b[c, a], cum_b[c, b])
            diff = cum2 - cum_rp[i:i + 1, c * 2 * CHUNK:(c + 1) * 2 * CHUNK]
            decay = jnp.where(incl, jnp.exp(jnp.where(incl, diff, 0.0)), 0.0)
            beta2 = jnp.where(lo_half, beta_b[c, a], beta_b[c, b])
            lmat.append(jnp.where(strict, beta2 * prod[:CHUNK] * decay, 0.0))
            pm = prod[CHUNK:] * decay
            pmat.append((jnp.where(lo_half, pm, 0.0).astype(BF16),
                         jnp.where(lo_half, 0.0, pm).astype(BF16)))

    rhs, kdec, qdec, elast = [], [], [], []
    for c in range(DELTA_CHUNKS):
        rsl = slice(c * CHUNK, (c + 1) * CHUNK)
        for h in range(A_HEADS):
            kc = kn[h][rsl]
            cum = cum_b[c, h]
            beta = beta_b[c, h]
            eg = jnp.exp(cum)
            g_last = cum[CHUNK - 1:CHUNK, :]
            rhs.append(_rhs_split(jnp.concatenate([beta * vv[h][rsl], (beta * eg) * kc], axis=1)))
            qdec.append(qn[h][rsl] * eg)
            kdec.append((kc * jnp.exp(g_last - cum)).astype(BF16))
            elast.append(jnp.exp(g_last))

    def mul(xs, ys):
        return [_mm(_pair_lhs(x), _pair_rhs(y, lo_half)) for x, y in zip(xs, ys)]

    def mul2(xs, zs, ys):
        both = [_mm(jnp.concatenate([_pair_lhs(x), _pair_lhs(z)], axis=0), _pair_rhs(y, lo_half))
                for x, z, y in zip(xs, zs, ys)]
        return [b[:CHUNK] for b in both], [b[CHUNK:] for b in both]

    def split(ms):
        return [_split_hi_lo(m) for m in ms]

    l_s = split(lmat)
    d1 = [_pair_mask(s, same16) for s in l_s]
    d2 = split(mul(d1, d1))
    r0 = [(eye_bf - hi, -lo) for hi, lo in d1]
    r0f = [eye - jnp.where(same16, m, 0.0) for m in lmat]
    d4f, r0d2 = mul2(d2, r0, d2)
    d4 = split(d4f)
    r1f = [r + m for r, m in zip(r0f, r0d2)]
    r1 = split(r1f)
    d8f, r1d4 = mul2(d4, r1, d4)
    d8 = split(d8f)
    r2f = [r + m for r, m in zip(r1f, r1d4)]
    r2 = split(r2f)
    r3f = [r + m for r, m in zip(r2f, mul(r2, d8))]
    r3 = split(r3f)
    t1 = split(mul([_pair_mask(s, off16) for s in l_s], r3))
    r4f = [r - m for r, m in zip(r3f, mul(r3, t1))]
    r4 = split(r4f)
    t2 = split(mul([_pair_mask(s, jnp.logical_not(same32)) for s in l_s], r4))
    r5f = [r - m for r, m in zip(r4f, mul(r4, t2))]

    sol = []
    for idx, t_pair in enumerate(r5f):
        c, i = divmod(idx, npair)
        swapped = pltpu.roll(t_pair, CHUNK, axis=1)
        for h, t_dup in ((2 * i, jnp.where(lo_half, t_pair, swapped)),
                         (2 * i + 1, jnp.where(lo_half, swapped, t_pair))):
            sol.append((c * A_HEADS + h, _mm(_lhs_split(t_dup, lo_half), rhs[c * A_HEADS + h])))
    sol = [s for _, s in sorted(sol, key=lambda e: e[0])]

    lin, s_add, o_add = [], [], []
    for idx, wu in enumerate(sol):
        c, h = divmod(idx, A_HEADS)
        wu_bf = wu.astype(BF16)
        g = lax.dot_general(kdec[idx], wu_bf, (((0,), (0,)), ((), ())),
                            preferred_element_type=F32)
        pm = pmat[c * npair + h // 2][h % 2]
        pw = jnp.dot(pm, jnp.concatenate([wu_bf, wu_bf], axis=0),
                     preferred_element_type=F32)
        lin.append(jnp.concatenate([g[:, A_DK:], qdec[idx] - pw[:, A_DK:]], axis=0).astype(BF16))
        s_add.append(g[:, :A_DK])
        o_add.append(pw[:, :A_DK])

    state = [s_ref[h] for h in range(A_HEADS)]
    for c in range(DELTA_CHUNKS):
        rsl = slice(c * CHUNK, (c + 1) * CHUNK)
        for h in range(A_HEADS):
            i = c * A_HEADS + h
            lanes = slice(h * A_DK, (h + 1) * A_DK)
            y = jnp.dot(lin[i], state[h].astype(BF16), preferred_element_type=F32)
            state[h] = state[h] * elast[i] - y[:A_DK] + s_add[i]
            o = y[A_DK:] + o_add[i]
            on = _rmsnorm(o, hn_ref[...])
            z = za_ref[rsl, lanes].astype(F32)
            o_ref[rsl, lanes] = (on * (z * _sigmoid(z))).astype(o_ref.dtype)
    for h in range(A_HEADS):
        s_ref[h] = state[h]


def _delta(p_main, ab, alt, alog_r, dt_r, alog_c, dt_c, head_norm, masks, batch, nc):
    t = p_main.shape[0]
    rows = DELTA_ROWS
    tril3, tri_even3, tri_odd3 = masks
    return pl.pallas_call(
        _delta_kernel,
        grid=(batch, nc),
        in_specs=[
            pl.BlockSpec((rows, 3 * A_W), lambda b, n: (b * nc + n, COL_QKV_A)),
            pl.BlockSpec((rows, A_W), lambda b, n: (b * nc + n, COL_ZA)),
            pl.BlockSpec((rows, AB_PAD), lambda b, n: (b * nc + n, 0)),
            pl.BlockSpec((1, A_HEADS, rows), lambda b, n: (b * nc + n, 0, 0)),
            _const_spec((1, AB_PAD)),
            _const_spec((1, AB_PAD)),
            _const_spec((A_HEADS, 1)),
            _const_spec((A_HEADS, 1)),
            _const_spec((1, A_DK)),
            _const_spec(tril3.shape),
            _const_spec(tri_even3.shape),
            _const_spec(tri_odd3.shape),
        ],
        out_specs=pl.BlockSpec((rows, A_W), lambda b, n: (b * nc + n, 0)),
        out_shape=jax.ShapeDtypeStruct((t, A_W), BF16),
        scratch_shapes=[pltpu.VMEM((A_HEADS, A_DK, A_DK), F32)],
        compiler_params=_params(("parallel", "arbitrary")),
        name="delta",
    )(p_main, p_main, ab, alt, alog_r, dt_r, alog_c, dt_c, head_norm, tril3, tri_even3, tri_odd3)


def _band_bias_kernel(win_ref, o_ref):
    rows = o_ref.shape[0] // CHUNK
    shape = (rows, CHUNK, 2 * CHUNK)
    i = lax.broadcasted_iota(jnp.int32, shape, 0) + pl.program_id(0) * rows
    j = lax.broadcasted_iota(jnp.int32, shape, 1)
    cand = lax.broadcasted_iota(jnp.int32, shape, 2)
    onehot = (i - j + (CHUNK - 1) == cand).astype(BF16).reshape(rows * CHUNK, 2 * CHUNK)
    acc = None
    for piece in _split3(win_ref[...]):
        part = jnp.dot(onehot, piece, preferred_element_type=F32)
        acc = part if acc is None else acc + part
    o_ref[...] = acc * LOG2E


def _band_bias(rel_bias):
    nslot = BAND_CHUNKS + 1
    relt = jnp.pad(rel_bias.T, ((0, nslot * CHUNK + CHUNK - N_REL), (0, 0)), mode="edge")
    win = jnp.concatenate([relt[(BAND_CHUNKS - w) * CHUNK:(BAND_CHUNKS - w + 2) * CHUNK]
                           for w in range(nslot)], axis=1)
    ncol = 2 * 2 * CHUNK
    win = jnp.pad(win, ((0, 0), (0, ncol - nslot * B_HEADS)))
    rows = 16
    out = pl.pallas_call(
        _band_bias_kernel,
        grid=(CHUNK // rows,),
        in_specs=[_const_spec((2 * CHUNK, ncol))],
        out_specs=pl.BlockSpec((rows * CHUNK, ncol), lambda g: (g, 0)),
        out_shape=jax.ShapeDtypeStruct((CHUNK * CHUNK, ncol), F32),
        compiler_params=_params(("parallel",)),
        name="band_bias",
    )(win)
    out = out[:, :nslot * B_HEADS].reshape(CHUNK, CHUNK, nslot, B_HEADS)
    return out.transpose(3, 0, 2, 1).reshape(B_HEADS, CHUNK, BAND_LEN)


def _band_kernel(q_ref, kp_ref, kc_ref, vp_ref, vc_ref, bias_ref, o_ref):
    blk = pl.program_id(1)

    def window(prev_ref, cur_ref, r0, lanes):
        return jnp.concatenate([prev_ref[r0:, lanes], cur_ref[:r0 + 2 * CHUNK, lanes]], axis=0)

    col_chunk = lax.broadcasted_iota(jnp.int32, (4 * CHUNK, BAND_WIN), 1) >> 6
    lane = lax.broadcasted_iota(jnp.int32, (CHUNK, 2 * B_DH), 1)
    even = lane < B_DH
    pairs = B_HEADS // 2

    def body(a2, masked):
        r0 = a2 * 2 * CHUNK
        for g0 in range(0, pairs, BAND_GROUP):
            sc = []
            for pr in range(g0, g0 + BAND_GROUP):
                lanes = slice(pr * 2 * B_DH, (pr + 1) * 2 * B_DH)
                qa = q_ref[r0:r0 + 2 * CHUNK, lanes]
                q4 = jnp.concatenate(
                    [jnp.where(even, qa[:CHUNK], 0), jnp.where(even, 0, qa[:CHUNK]),
                     jnp.where(even, qa[CHUNK:], 0), jnp.where(even, 0, qa[CHUNK:])], axis=0)
                kp = window(kp_ref, kc_ref, r0, lanes)
                s = lax.dot_general(q4, kp, (((1,), (1,)), ((), ())),
                                    preferred_element_type=F32) + bias_ref[pr]
                if masked:
                    s = jnp.where(col_chunk + 2 * a2 >= BAND_CHUNKS, s, -1e30)
                sc.append(s)
            m = [jnp.max(s, axis=-1, keepdims=True) for s in sc]
            p = [jnp.exp2(s - mm) for s, mm in zip(sc, m)]
            denom = [jnp.sum(pp, axis=-1, keepdims=True) for pp in p]
            for i, pr in enumerate(range(g0, g0 + BAND_GROUP)):
                lanes = slice(pr * 2 * B_DH, (pr + 1) * 2 * B_DH)
                vp = window(vp_ref, vc_ref, r0, lanes)
                o4 = jnp.dot(p[i].astype(BF16), vp, preferred_element_type=F32) / denom[i]
                o2 = jnp.concatenate(
                    [jnp.where(even, o4[:CHUNK], o4[CHUNK:2 * CHUNK]),
                     jnp.where(even, o4[2 * CHUNK:3 * CHUNK], o4[3 * CHUNK:])], axis=0)
                o_ref[r0:r0 + 2 * CHUNK, lanes] = o2.astype(o_ref.dtype)

    steps = QBLK // (2 * CHUNK)

    @pl.when(blk == 0)
    def _():
        for a2 in range(steps):
            body(a2, masked=True)

    @pl.when(blk > 0)
    def _():
        for a2 in range(steps):
            body(a2, masked=False)


def _band_tables(bias):
    neg = jnp.full((B_HEADS, CHUNK, CHUNK), -1e30, F32)
    first = jnp.concatenate([bias, neg], axis=2)
    second = jnp.concatenate([neg, bias], axis=2)
    t = jnp.stack([first[0::2], first[1::2], second[0::2], second[1::2]], axis=1)
    return t.reshape(B_HEADS // 2, 4 * CHUNK, BAND_WIN)


def _band(p_main, bias, batch, nblk):
    t = p_main.shape[0]

    def prev(b, i):
        return b * nblk + jnp.maximum(i - 1, 0)

    return pl.pallas_call(
        _band_kernel,
        grid=(batch, nblk),
        in_specs=[
            pl.BlockSpec((QBLK, D_MODEL), lambda b, i: (b * nblk + i, COL_QB)),
            pl.BlockSpec((QBLK, D_MODEL), lambda b, i: (prev(b, i), COL_KB)),
            pl.BlockSpec((QBLK, D_MODEL), lambda b, i: (b * nblk + i, COL_KB)),
            pl.BlockSpec((QBLK, D_MODEL), lambda b, i: (prev(b, i), COL_VB)),
            pl.BlockSpec((QBLK, D_MODEL), lambda b, i: (b * nblk + i, COL_VB)),
            _const_spec((B_HEADS // 2, 4 * CHUNK, BAND_WIN)),
        ],
        out_specs=pl.BlockSpec((QBLK, D_MODEL), lambda b, i: (b * nblk + i, 0)),
        out_shape=jax.ShapeDtypeStruct((t, D_MODEL), BF16),
        compiler_params=_params(("parallel", "parallel")),
        name="band",
    )(p_main, p_main, p_main, p_main, p_main, bias)


def _merge_kernel(oa_ref, ob_ref, ga_ref, gb_ref, x_ref, wa_ref, wb_ref, wo_ref, o_ref):
    ya = jnp.dot(oa_ref[...], wa_ref[...], preferred_element_type=F32)
    yb = jnp.dot(ob_ref[...], wb_ref[...], preferred_element_type=F32)
    y = _sigmoid(ga_ref[...].astype(F32)) * ya + _sigmoid(gb_ref[...].astype(F32)) * yb
    o_ref[...] = x_ref[...] + jnp.dot(y.astype(BF16), wo_ref[...], preferred_element_type=F32)


def _merge(oa, ob, p_main, x, wa, wb, wo, tm, l):
    t = x.shape[0]
    row = lambda i: (i, 0)
    return pl.pallas_call(
        _merge_kernel,
        grid=(t // tm,),
        in_specs=[
            pl.BlockSpec((tm, D_MODEL), row),
            pl.BlockSpec((tm, D_MODEL), row),
            pl.BlockSpec((tm, D_MODEL), lambda i: (i, COL_GA)),
            pl.BlockSpec((tm, D_MODEL), lambda i: (i, COL_GB)),
            pl.BlockSpec((tm, D_MODEL), row),
            _layer_spec((D_MODEL, D_MODEL), l),
            _layer_spec((D_MODEL, D_MODEL), l),
            _layer_spec((D_MODEL, D_MODEL), l),
        ],
        out_specs=pl.BlockSpec((tm, D_MODEL), row),
        out_shape=jax.ShapeDtypeStruct((t, D_MODEL), F32),
        compiler_params=_params(("parallel",)),
        name="merge",
    )(oa, ob, p_main, p_main, x, wa, wb, wo)


def _xattn_kernel(x_ref, g_ref, wq_ref, kv_ref, wo_ref, o_ref):
    x = x_ref[...]
    h = _rmsnorm(x, g_ref[...]).astype(BF16)
    q = jnp.dot(h, wq_ref[...], preferred_element_type=F32).astype(BF16)
    scale = M_DH ** -0.5
    outs = []
    for hh in range(M_HEADS):
        kh = kv_ref[:, hh * M_DH:(hh + 1) * M_DH]
        vh = kv_ref[:, D_MODEL + hh * M_DH:D_MODEL + (hh + 1) * M_DH]
        sc = _dot_nt(q[:, hh * M_DH:(hh + 1) * M_DH], kh) * scale
        m = jnp.max(sc, axis=-1, keepdims=True)
        p = jnp.exp(sc - m)
        denom = jnp.sum(p, axis=-1, keepdims=True)
        outs.append((_dot(p, vh) / denom).astype(BF16))
    o = jnp.concatenate(outs, axis=1)
    o_ref[...] = x + jnp.dot(o, wo_ref[...], preferred_element_type=F32)


def _xattn(x, g, wq, kv, wo, tm, seq, mem_len, l):
    t = x.shape[0]
    per_batch = seq // tm
    return pl.pallas_call(
        _xattn_kernel,
        grid=(t // tm,),
        in_specs=[
            pl.BlockSpec((tm, D_MODEL), lambda i: (i, 0)),
            _layer_spec((1, D_MODEL), l),
            _layer_spec((D_MODEL, D_MODEL), l),
            pl.BlockSpec((mem_len, 2 * D_MODEL), lambda i: (i // per_batch, 0)),
            _layer_spec((D_MODEL, D_MODEL), l),
        ],
        out_specs=pl.BlockSpec((tm, D_MODEL), lambda i: (i, 0)),
        out_shape=jax.ShapeDtypeStruct((t, D_MODEL), F32),
        compiler_params=_params(("parallel",)),
        name="xattn",
    )(x, g, wq, kv, wo)


FF_CHUNK = 256


def _ffn_kernel(x_ref, g_ref, wgu_ref, wd_ref, gf_ref, o_ref, act_ref, *, final_norm):
    x = x_ref[...]
    h = _rmsnorm(x, g_ref[...]).astype(BF16)
    for c in range(D_FF // FF_CHUNK):
        lo = c * FF_CHUNK
        gate = jnp.dot(h, wgu_ref[:, lo:lo + FF_CHUNK], preferred_element_type=F32)
        up = jnp.dot(h, wgu_ref[:, D_FF + lo:D_FF + lo + FF_CHUNK], preferred_element_type=F32)
        act_ref[:, lo:lo + FF_CHUNK] = (gate * _sigmoid(gate) * up).astype(BF16)
    y = x + jnp.dot(act_ref[...], wd_ref[...], preferred_element_type=F32)
    if final_norm:
        y = _rmsnorm(y, gf_ref[...])
    o_ref[...] = y


def _ffn(x, g, wgu, wd, gf, tm, final_norm, l):
    t = x.shape[0]
    return pl.pallas_call(
        functools.partial(_ffn_kernel, final_norm=final_norm),
        grid=(t // tm,),
        in_specs=[
            pl.BlockSpec((tm, D_MODEL), lambda i: (i, 0)),
            _layer_spec((1, D_MODEL), l),
            _layer_spec((D_MODEL, 2 * D_FF), l),
            _layer_spec((D_FF, D_MODEL), l),
            _const_spec((1, D_MODEL)),
        ],
        out_specs=pl.BlockSpec((tm, D_MODEL), lambda i: (i, 0)),
        out_shape=jax.ShapeDtypeStruct((t, D_MODEL), F32),
        scratch_shapes=[pltpu.VMEM((tm, D_FF), BF16)],
        compiler_params=_params(("parallel",)),
        name="ffn",
    )(x, g, wgu, wd, gf)


def kernel(x, mem, norm_mix, w_in, conv_w, a_log, dt_bias, head_norm, w_a_out, w_b_out,
           rel_bias, w_o, norm_xattn, norm_mem, w_mq, w_mkv, w_mo, norm_ffn, w_gate_up,
           w_down, norm_final):
    batch, seq, d = x.shape
    mem_len = mem.shape[1]
    depth = w_in.shape[0]
    t = batch * seq
    nsteps = seq // DELTA_ROWS
    nblk = seq // QBLK
    tm = min(512, seq)

    xf = x.reshape(t, d)
    memf = mem.reshape(batch * mem_len, d)
    bias = _band_tables(_band_bias(rel_bias))
    masks = _cumsum_masks()

    ab_lo = 4 * A_W
    ab_hi = ab_lo + 2 * A_HEADS
    w_lo, w_hi = w_in[:, :, :ab_lo].astype(BF16), w_in[:, :, ab_hi:].astype(BF16)
    w_ab = jnp.pad(w_in[:, :, ab_lo:ab_hi],
                   ((0, 0), (0, 0), (0, AB_PAD - 2 * A_HEADS))).astype(BF16)
    wa, wb, wo = w_a_out.astype(BF16), w_b_out.astype(BF16), w_o.astype(BF16)
    wmq, wmkv, wmo = w_mq.astype(BF16), w_mkv.astype(BF16), w_mo.astype(BF16)
    wgu, wd = w_gate_up.astype(BF16), w_down.astype(BF16)

    def rows3(v):
        return v.reshape(depth, 1, -1)

    g_mix, g_x, g_mem, g_ffn = rows3(norm_mix), rows3(norm_xattn), rows3(norm_mem), rows3(norm_ffn)

    def pad_lanes(v):
        return jnp.pad(v.reshape(1, -1), ((0, 0), (0, AB_PAD - v.shape[-1])))

    def even_odd_col(v):
        return jnp.concatenate([v[0::2], v[1::2]]).reshape(A_HEADS, 1)

    for l in range(depth):
        p_main, ab = _inproj(xf, g_mix, w_lo, w_hi, w_ab, conv_w, tm, seq, l)

        alt = ab[:, :A_HEADS].reshape(batch * nsteps, DELTA_ROWS, A_HEADS).transpose(0, 2, 1)
        alt = jnp.concatenate([alt[:, 0::2], alt[:, 1::2]], axis=1)
        oa = _delta(p_main, ab, alt, pad_lanes(a_log[l]), pad_lanes(dt_bias[l]),
                    even_odd_col(a_log[l]), even_odd_col(dt_bias[l]),
                    head_norm[l].reshape(1, A_DK), masks, batch, nsteps)
        ob = _band(p_main, bias, batch, nblk)
        x1 = _merge(oa, ob, p_main, xf, wa, wb, wo, tm, l)

        kv = _norm_matmul(memf, g_mem, wmkv, min(512, batch * mem_len), l)
        x2 = _xattn(x1, g_x, wmq, kv, wmo, tm, seq, mem_len, l)

        xf = _ffn(x2, g_ffn, wgu, wd, norm_final.reshape(1, d), tm,
                  final_norm=(l == depth - 1), l=l)
    return xf.reshape(batch, seq, d)
```
